```python
import jax, jax.numpy as jnp
from jax import lax
import numpy as np

D_MODEL = 1024
BATCH = 8
SEQ = 8192
DEPTH = 2
DEC_BATCH = 16
DEC_SEQ = 2048
PAST_LEN = 128

HEAD_DIM = 64
A_HEADS = 6
A_WIDTH = A_HEADS * HEAD_DIM
B_HEADS = 6
B_WIDTH = B_HEADS * HEAD_DIM
C_HEADS = 4
C_KV_HEADS = 2
C_GROUP = C_HEADS // C_KV_HEADS
C_WIDTH = C_HEADS * HEAD_DIM
C_KV_WIDTH = C_KV_HEADS * HEAD_DIM
MIX_WIDTH = A_WIDTH + B_WIDTH + C_WIDTH
CONV_K = 5
CHUNK = 64
WINDOW = 128
BLOCK = 128
PLE_DIM = 256
EPS = 1e-6
IN_COLS = (3 * A_WIDTH, A_WIDTH, 2 * A_HEADS, 2 * A_HEADS,
           B_WIDTH, B_WIDTH, B_WIDTH, B_WIDTH,
           C_WIDTH, C_KV_WIDTH, C_KV_WIDTH, C_WIDTH)
IN_WIDTH = sum(IN_COLS)

kernel_name = 'hybrid_bidir_deltanet_retention_swa_encoder'


def rms_norm(x, g):
    xf = x.astype(jnp.float32)
    y = xf * lax.rsqrt(jnp.mean(xf * xf, axis=-1, keepdims=True) + EPS)
    return (y * g.astype(jnp.float32)).astype(x.dtype)


def l2_normalize(x):
    return x * lax.rsqrt(jnp.sum(x * x, axis=-1, keepdims=True) + EPS)


def head_layer_norm(x, g):
    mu = jnp.mean(x, axis=-1, keepdims=True)
    xc = x - mu
    y = xc * lax.rsqrt(jnp.mean(xc * xc, axis=-1, keepdims=True) + EPS)
    return y.reshape(x.shape[:2] + (-1,)) * g


def centred_dwconv(x, w):
    return lax.conv_general_dilated(
        x, w[:, None, :], window_strides=(1,),
        padding=[(CONV_K // 2, CONV_K // 2)],
        dimension_numbers=('NWC', 'WIO', 'NWC'),
        feature_group_count=x.shape[-1])


def to_chunks(a, size):
    b, t, h = a.shape[:3]
    a = a.reshape((b, t // size, size, h) + a.shape[3:])
    return jnp.moveaxis(a, (1, 3), (0, 2))


def from_chunks(a):
    n, b, h, c = a.shape[:4]
    a = jnp.moveaxis(a, (0, 2), (1, 3))
    return a.reshape((b, n * c, h) + a.shape[4:])


def flip_t(a):
    return jnp.flip(a, axis=1)


def gated_delta_rule(q, k, v, beta, g):
    b, t, h, dk = q.shape
    dv = v.shape[-1]
    q = to_chunks(q, CHUNK) * (dk ** -0.5)
    k = to_chunks(k, CHUNK)
    v = to_chunks(v, CHUNK)
    beta = to_chunks(beta, CHUNK)
    decay = jnp.cumsum(to_chunks(g, CHUNK), axis=-1)
    idx = jnp.arange(CHUNK)
    incl = idx[:, None] >= idx[None, :]
    strict = idx[:, None] > idx[None, :]
    gamma = jnp.exp(jnp.where(incl, decay[..., :, None] - decay[..., None, :], -jnp.inf))
    k_beta = k * beta[..., None]
    a_mat = jnp.einsum('nbhcd,nbhsd->nbhcs', k_beta, k) * gamma
    t_mat = jnp.where(strict, a_mat, 0.0) + jnp.eye(CHUNK, dtype=q.dtype)
    rhs = jnp.concatenate([v * beta[..., None], k_beta * jnp.exp(decay)[..., None]], axis=-1)
    sol = lax.linalg.triangular_solve(t_mat, rhs, left_side=True, lower=True, unit_diagonal=True)
    u, w = sol[..., :dv], sol[..., dv:]
    qk = jnp.einsum('nbhcd,nbhsd->nbhcs', q, k) * gamma
    q_dec = q * jnp.exp(decay)[..., None]
    k_dec = k * jnp.exp(decay[..., -1:] - decay)[..., None]
    chunk_decay = jnp.exp(decay[..., -1])[..., None, None]

    def step(state, inp):
        u_c, w_c, qk_c, qd_c, kd_c, cd_c = inp
        v_new = u_c - jnp.einsum('bhcd,bhde->bhce', w_c, state)
        o = jnp.einsum('bhcd,bhde->bhce', qd_c, state) + jnp.einsum('bhcs,bhse->bhce', qk_c, v_new)
        state = state * cd_c + jnp.einsum('bhcd,bhce->bhde', kd_c, v_new)
        return state, o

    s0 = jnp.zeros((b, h, dk, dv), q.dtype)
    _, o = lax.scan(step, s0, (u, w, qk, q_dec, k_dec, chunk_decay))
    return from_chunks(o)


def retention(q, k, v, log_gamma):
    b, t, h, dk = q.shape
    dv = v.shape[-1]
    q = to_chunks(q, CHUNK)
    k = to_chunks(k, CHUNK) * (dk ** -0.5)
    v = to_chunks(v, CHUNK)
    pos = jnp.arange(CHUNK, dtype=jnp.float32)
    idx = jnp.arange(CHUNK)
    incl = idx[:, None] >= idx[None, :]
    lg = log_gamma[:, None]
    intra_decay = jnp.exp(jnp.where(incl[None], (pos[:, None] - pos[None, :])[None] * lg[:, :, None], -jnp.inf))
    o_intra = jnp.einsum('nbhcs,nbhse->nbhce', jnp.einsum('nbhcd,nbhsd->nbhcs', q, k) * intra_decay, v)
    q_dec = q * jnp.exp((pos[None, :] + 1.0) * lg)[..., None]
    k_dec = k * jnp.exp((CHUNK - 1.0 - pos[None, :]) * lg)[..., None]
    chunk_decay = jnp.exp(CHUNK * log_gamma)[:, None, None]

    def step(state, inp):
        qd_c, kd_c, v_c = inp
        o = jnp.einsum('bhcd,bhde->bhce', qd_c, state)
        state = state * chunk_decay + jnp.einsum('bhcd,bhce->bhde', kd_c, v_c)
        return state, o

    s0 = jnp.zeros((b, h, dk, dv), q.dtype)
    _, o_inter = lax.scan(step, s0, (q_dec, k_dec, v))
    return from_chunks(o_intra + o_inter)


def window_attention(q, k, v, sink):
    b, t, _, d = q.shape
    nb = t // BLOCK
    qb = q.reshape(b, nb, BLOCK, C_KV_HEADS, C_GROUP, d)

    def band(a):
        ap = jnp.pad(a, ((0, 0), (BLOCK, BLOCK), (0, 0), (0, 0))).reshape(b, nb + 2, BLOCK, C_KV_HEADS, d)
        return jnp.concatenate([ap[:, :-2], ap[:, 1:-1], ap[:, 2:]], axis=2)

    kb, vb = band(k), band(v)
    scores = jnp.einsum('bnqhgd,bnkhd->bnhgqk', qb, kb) * (d ** -0.5)
    qi = jnp.arange(BLOCK)
    kj = jnp.arange(3 * BLOCK)
    rel = kj[None, :] - BLOCK - qi[:, None]
    key_pos = jnp.arange(nb)[:, None] * BLOCK - BLOCK + kj[None, :]
    mask = (jnp.abs(rel) <= WINDOW)[None] & ((key_pos >= 0) & (key_pos < t))[:, None, :]
    slopes = (2.0 ** (-8.0 * jnp.arange(1, C_HEADS + 1, dtype=jnp.float32) / C_HEADS)).reshape(C_KV_HEADS, C_GROUP)
    alibi = -slopes[:, :, None, None] * jnp.abs(rel).astype(jnp.float32)[None, None]
    scores = jnp.where(mask[None, :, None, None], scores + alibi, -jnp.inf)
    sink = sink.reshape(C_KV_HEADS, C_GROUP)[:, :, None, None]
    m = jnp.maximum(jnp.max(scores, axis=-1, keepdims=True), sink)
    pr = jnp.exp(scores - m)
    pr = pr / (jnp.sum(pr, axis=-1, keepdims=True) + jnp.exp(sink - m))
    o = jnp.einsum('bnhgqk,bnkhd->bnqhgd', pr, vb)
    return o.reshape(b, t, C_WIDTH)


def encoder_layer(h, p_i, w_in, w_out, norm_g, conv_w, dn_a_log, dn_dt_bias, dn_norm_g,
                  ret_decay_z, ret_norm_g, attn_sink, w_ple, w_pg):
    f32 = jnp.float32
    b, t, _ = h.shape
    xn = rms_norm(h, norm_g)
    proj = jnp.matmul(xn, w_in).astype(f32)
    splits = [int(s) for s in np.cumsum(IN_COLS)[:-1]]
    (a_qkv, a_gate, a_beta, a_alpha, b_q, b_k, b_v, b_gate,
     c_q, c_k, c_v, c_gate) = jnp.split(proj, splits, axis=-1)

    def heads(z, n):
        return z.reshape(b, t, n, HEAD_DIM)

    qkv = jax.nn.silu(centred_dwconv(a_qkv, conv_w.astype(f32)))
    aq, ak, av = jnp.split(qkv, 3, axis=-1)
    aq = l2_normalize(heads(aq, A_HEADS))
    ak = l2_normalize(heads(ak, A_HEADS))
    av = heads(av, A_HEADS)
    beta = jax.nn.sigmoid(a_beta.reshape(b, t, 2, A_HEADS))
    g = -jnp.exp(dn_a_log.astype(f32)) * jax.nn.softplus(a_alpha.reshape(b, t, 2, A_HEADS) + dn_dt_bias.astype(f32))
    o_fwd = gated_delta_rule(aq, ak, av, beta[:, :, 0], g[:, :, 0])
    o_bwd = flip_t(gated_delta_rule(flip_t(aq), flip_t(ak), flip_t(av), flip_t(beta[:, :, 1]), flip_t(g[:, :, 1])))
    o_a = (rms_norm(o_fwd + o_bwd, dn_norm_g) * jax.nn.silu(heads(a_gate, A_HEADS))).reshape(b, t, A_WIDTH)

    log_gamma = jax.nn.log_sigmoid(ret_decay_z.astype(f32))
    bq, bk, bv = heads(b_q, B_HEADS), heads(b_k, B_HEADS), heads(b_v, B_HEADS)
    r = retention(bq, bk, bv, log_gamma[0]) + flip_t(retention(flip_t(bq), flip_t(bk), flip_t(bv), log_gamma[1]))
    o_b = head_layer_norm(r, ret_norm_g.astype(f32)) * jax.nn.silu(b_gate)

    o_c = window_attention(heads(c_q, C_HEADS), heads(c_k, C_KV_HEADS), heads(c_v, C_KV_HEADS),
                           attn_sink.astype(f32)) * jax.nn.silu(c_gate)

    mix = jnp.concatenate([o_a, o_b, o_c], axis=-1).astype(h.dtype)
    h = h + jnp.matmul(mix, w_out)
    h = h + jnp.matmul(p_i, w_ple) * jax.nn.sigmoid(jnp.matmul(h, w_pg))
    return h


def encoder_trunk(x, p, w_in, w_out, norm_g, conv_w, dn_a_log, dn_dt_bias, dn_norm_g,
                  ret_decay_z, ret_norm_g, attn_sink, w_ple, w_pg, final_g):
    h = x
    for i in range(DEPTH):
        h = encoder_layer(h, p[i], w_in[i], w_out[i], norm_g[i], conv_w[i], dn_a_log[i], dn_dt_bias[i],
                          dn_norm_g[i], ret_decay_z[i], ret_norm_g[i], attn_sink[i], w_ple[i], w_pg[i])
    return rms_norm(h, final_g)


def setup_inputs(seed: int = 0) -> dict:
    key = jax.random.key(seed)
    ks = jax.random.split(key, 18)
    f32 = jnp.float32

    def nrm(k, shape, scale):
        return jax.random.normal(k, shape, f32) * scale

    h_idx = jnp.arange(B_HEADS, dtype=f32)
    ret_gamma = 1.0 - 2.0 ** (-5.0 - h_idx)
    ret_z0 = jnp.log(ret_gamma) - jnp.log1p(-ret_gamma)
    dt = jnp.exp(jax.random.uniform(ks[8], (DEPTH, 2, A_HEADS), f32) * (jnp.log(0.1) - jnp.log(0.001)) + jnp.log(0.001))
    return {
        'x_prompt': nrm(ks[0], (BATCH, SEQ, D_MODEL), 1.0),
        'x_sample': nrm(ks[1], (DEC_BATCH, DEC_SEQ, D_MODEL), 1.0),
        'p_prompt': nrm(ks[2], (DEPTH, BATCH, SEQ, PLE_DIM), 1.0),
        'p_sample': nrm(ks[3], (DEPTH, DEC_BATCH, DEC_SEQ, PLE_DIM), 1.0),
        'w_in': nrm(ks[4], (DEPTH, D_MODEL, IN_WIDTH), D_MODEL ** -0.5),
        'w_out': nrm(ks[5], (DEPTH, MIX_WIDTH, D_MODEL), MIX_WIDTH ** -0.5),
        'norm_g': 1.0 + nrm(ks[6], (DEPTH, D_MODEL), 0.05),
        'conv_w': nrm(ks[7], (DEPTH, CONV_K, 3 * A_WIDTH), CONV_K ** -0.5),
        'dn_a_log': jnp.log(jax.random.uniform(ks[9], (DEPTH, 2, A_HEADS), f32, 1.0, 16.0)),
        'dn_dt_bias': dt + jnp.log(-jnp.expm1(-dt)),
        'dn_norm_g': 1.0 + nrm(ks[10], (DEPTH, HEAD_DIM), 0.05),
        'ret_decay_z': ret_z0 + nrm(ks[11], (DEPTH, 2, B_HEADS), 0.1),
        'ret_norm_g': 1.0 + nrm(ks[12], (DEPTH, B_WIDTH), 0.05),
        'attn_sink': nrm(ks[13], (DEPTH, C_HEADS), 0.5),
        'w_ple': nrm(ks[14], (DEPTH, PLE_DIM, D_MODEL), PLE_DIM ** -0.5),
        'w_pg': nrm(ks[15], (DEPTH, D_MODEL, D_MODEL), D_MODEL ** -0.5),
        'final_g': 1.0 + nrm(ks[16], (D_MODEL,), 0.05),
    }


def reference(x_prompt, x_sample, p_prompt, p_sample, w_in, w_out, norm_g, conv_w, dn_a_log, dn_dt_bias,
              dn_norm_g, ret_decay_z, ret_norm_g, attn_sink, w_ple, w_pg, final_g):
    y_prompt = encoder_trunk(x_prompt, p_prompt, w_in, w_out, norm_g, conv_w, dn_a_log, dn_dt_bias,
                             dn_norm_g, ret_decay_z, ret_norm_g, attn_sink, w_ple, w_pg, final_g)
    y_sample = encoder_trunk(x_sample, p_sample, w_in, w_out, norm_g, conv_w, dn_a_log, dn_dt_bias,
                             dn_norm_g, ret_decay_z, ret_norm_g, attn_sink, w_ple, w_pg, final_g)
    return (y_prompt, y_sample)
```

```python
import functools

import jax
import jax.numpy as jnp
from jax import lax
from jax.experimental import pallas as pl
from jax.experimental.pallas import tpu as pltpu

F32 = jnp.float32
BF16 = jnp.bfloat16

D_MODEL = 1024
HEAD_DIM = 64
A_HEADS = 6
A_WIDTH = A_HEADS * HEAD_DIM
B_HEADS = 6
B_WIDTH = B_HEADS * HEAD_DIM
C_HEADS = 4
C_KV_HEADS = 2
C_GROUP = C_HEADS // C_KV_HEADS
C_WIDTH = C_HEADS * HEAD_DIM
C_KV_WIDTH = C_KV_HEADS * HEAD_DIM
CONV_K = 5
WINDOW = 128
PLE_DIM = 256
EPS = 1e-6
SCALE = HEAD_DIM ** -0.5

LANES = 128
SUBLANES = 8
VMEM_LIMIT = 48 * 1024 * 1024

QKV_W = 3 * A_WIDTH
P_AG = QKV_W
P_BQ = P_AG + A_WIDTH
P_BK = P_BQ + B_WIDTH
P_BV = P_BK + B_WIDTH
P_BG = P_BV + B_WIDTH
P_CQ = P_BG + B_WIDTH
P_CK = P_CQ + C_WIDTH
P_CV = P_CK + C_KV_WIDTH
P_CG = P_CV + C_KV_WIDTH
P_BA = P_CG + C_WIDTH
PROJ_W = P_BA + LANES
ORIG_BA = QKV_W + A_WIDTH
N_BA = 4 * A_HEADS
BA_ROWS = 32
G_OFF = 2 * A_HEADS

CHUNK = 64
RCHUNK = 128
QBLOCK = 128
TB = 512

NT_DIMS = (((1,), (1,)), ((), ()))
TN_DIMS = (((0,), (0,)), ((), ()))


def _dot(a, b):
    return jnp.dot(a, b, preferred_element_type=F32)


def _dot_nt(a, b):
    return lax.dot_general(a, b, NT_DIMS, preferred_element_type=F32)


def _dot_tn(a, b):
    return lax.dot_general(a, b, TN_DIMS, preferred_element_type=F32)


def _sigmoid(x):
    return 1.0 / (1.0 + jnp.exp(-x))


def _silu(x):
    return x * _sigmoid(x)


def _softplus(x):
    return jnp.maximum(x, 0.0) + jnp.log1p(jnp.exp(-jnp.abs(x)))


def _split3(x):
    hi = x.astype(BF16)
    r1 = x - hi.astype(F32)
    mid = r1.astype(BF16)
    lo = (r1 - mid.astype(F32)).astype(BF16)
    return hi, mid, lo


def _pair_masks(rows):
    lane = lax.broadcasted_iota(jnp.int32, (rows, LANES), 1)
    return lane < HEAD_DIM


def _params(sem):
    return pltpu.CompilerParams(dimension_semantics=sem, vmem_limit_bytes=VMEM_LIMIT)


def _in_proj_body(h_ref, g_ref, w_ref, wba_ref, proj_ref, gt_ref):
    x = h_ref[0]
    ms = jnp.mean(x * x, axis=-1, keepdims=True)
    xn = (x * lax.rsqrt(ms + EPS) * g_ref[...]).astype(BF16)
    for c0 in range(0, PROJ_W, 512):
        c1 = min(c0 + 512, PROJ_W)
        proj_ref[0, :, c0:c1] = _dot(xn, w_ref[:, c0:c1])
    gt_ref[0] = _dot_nt(wba_ref[...], xn)


def _in_proj(h, norm_g, w_p, w_ba_t):
    b, t, _ = h.shape
    return pl.pallas_call(
        _in_proj_body,
        grid=(b, t // TB),
        in_specs=[
            pl.BlockSpec((1, TB, D_MODEL), lambda bi, i: (bi, i, 0)),
            pl.BlockSpec((1, D_MODEL), lambda bi, i: (0, 0)),
            pl.BlockSpec((D_MODEL, PROJ_W), lambda bi, i: (0, 0)),
            pl.BlockSpec((BA_ROWS, D_MODEL), lambda bi, i: (0, 0)),
        ],
        out_specs=[
            pl.BlockSpec((1, TB, PROJ_W), lambda bi, i: (bi, i, 0)),
            pl.BlockSpec((1, BA_ROWS, TB), lambda bi, i: (bi, 0, i)),
        ],
        out_shape=[jax.ShapeDtypeStruct((b, t, PROJ_W), F32), jax.ShapeDtypeStruct((b, BA_ROWS, t), F32)],
        compiler_params=_params(("parallel", "parallel")),
        name="in_proj",
    )(h, norm_g, w_p, w_ba_t)


def _delta_body(nb, *refs):
    (main_f, prev_f, next_f, ba_f, gt_f, main_b, prev_b, next_b, ba_b, gt_b,
     cw_ref, acol_ref, dtcol_ref, arow_ref, dtrow_ref,
     of_ref, ob_ref, xs, ys, bcol, dcol, drow, st) = refs
    i = pl.program_id(1)
    nc = TB // CHUNK

    @pl.when(i == 0)
    def _():
        st[...] = jnp.zeros_like(st)

    ri = lax.broadcasted_iota(jnp.int32, (CHUNK, CHUNK), 0)
    ci = lax.broadcasted_iota(jnp.int32, (CHUNK, CHUNK), 1)
    incl = (ri >= ci, ri <= ci)
    strict = (ri > ci, ri < ci)
    lo_lane = _pair_masks(CHUNK)
    cw = cw_ref[...]

    ins = ((main_f, prev_f, next_f, ba_f, gt_f, of_ref), (main_b, prev_b, next_b, ba_b, gt_b, ob_ref))
    for dirn, (main, prev, nxt, ba, gt, _) in enumerate(ins):
        blk = i if dirn == 0 else nb - 1 - i
        xs[dirn, 0:SUBLANES, :] = jnp.where(blk > 0, prev[0], 0.0)
        xs[dirn, SUBLANES:TB + SUBLANES, :] = main[0]
        xs[dirn, TB + SUBLANES:TB + 2 * SUBLANES, :] = jnp.where(blk < nb - 1, nxt[0], 0.0)
        cum = incl[dirn].astype(BF16)
        cum_t = incl[1 - dirn].astype(BF16)
        for c in range(nc):
            r0 = c * CHUNK
            base = r0 + SUBLANES - CONV_K // 2
            acc = cw[0:1, :] * xs[dirn, base:base + CHUNK, :]
            for k in range(1, CONV_K):
                acc = acc + cw[k:k + 1, :] * xs[dirn, base + k:base + k + CHUNK, :]
            y = _silu(acc)
            for p in range(2 * A_WIDTH // LANES):
                blk128 = y[:, p * LANES:(p + 1) * LANES]
                sq = blk128 * blk128
                s_lo = jnp.sum(jnp.where(lo_lane, sq, 0.0), axis=-1, keepdims=True)
                s_hi = jnp.sum(jnp.where(lo_lane, 0.0, sq), axis=-1, keepdims=True)
                r = jnp.where(lo_lane, lax.rsqrt(s_lo + EPS), lax.rsqrt(s_hi + EPS))
                ys[dirn, r0:r0 + CHUNK, p * LANES:(p + 1) * LANES] = blk128 * r
            ys[dirn, r0:r0 + CHUNK, 2 * A_WIDTH:QKV_W] = y[:, 2 * A_WIDTH:QKV_W]
            bac = ba[0, r0:r0 + CHUNK, :]
            bcol[dirn, r0:r0 + CHUNK, :] = _sigmoid(bac)
            g_c = -acol_ref[...] * _softplus(bac + dtcol_ref[...])
            h3, m3, l3 = _split3(g_c)
            d3 = _dot(cum, jnp.concatenate([h3, m3, l3], axis=-1))
            dcol[dirn, r0:r0 + CHUNK, :] = (d3[:, 0:LANES] + d3[:, LANES:2 * LANES]) + d3[:, 2 * LANES:3 * LANES]
            gtc = gt[0, :, r0:r0 + CHUNK]
            g_r = -arow_ref[...] * _softplus(gtc + dtrow_ref[...])
            h3, m3, l3 = _split3(g_r)
            d3 = _dot(jnp.concatenate([h3, m3, l3], axis=0), cum_t)
            drow[dirn, c] = (d3[0:BA_ROWS] + d3[BA_ROWS:2 * BA_ROWS]) + d3[2 * BA_ROWS:3 * BA_ROWS]

    def chunk_step(c, carry):
        for dirn, (_, _, _, _, _, o_ref) in enumerate(ins):
            cc = c if dirn == 0 else nc - 1 - c
            r0 = pl.multiple_of(cc * CHUNK, CHUNK)
            beta_all = bcol[dirn, pl.ds(r0, CHUNK), :]
            dcol_all = dcol[dirn, pl.ds(r0, CHUNK), :]
            drow_all = drow[dirn, cc]
            for h in range(A_HEADS):
                lb = A_HEADS * dirn + h
                lg = G_OFF + lb
                beta = beta_all[:, lb:lb + 1]
                dc = dcol_all[:, lg:lg + 1]
                dr = drow_all[lg:lg + 1, :]
                gam = jnp.where(incl[dirn], jnp.exp(dc - dr), 0.0)
                q = ys[dirn, pl.ds(r0, CHUNK), h * HEAD_DIM:(h + 1) * HEAD_DIM]
                k = ys[dirn, pl.ds(r0, CHUNK), A_WIDTH + h * HEAD_DIM:A_WIDTH + (h + 1) * HEAD_DIM]
                v = ys[dirn, pl.ds(r0, CHUNK), 2 * A_WIDTH + h * HEAD_DIM:2 * A_WIDTH + (h + 1) * HEAD_DIM]
                qb = q.astype(BF16)
                kb = k.astype(BF16)
                e_dc = jnp.exp(dc)
                m = -jnp.where(strict[dirn], _dot_nt(kb, kb) * beta * gam, 0.0)
                qk = _dot_nt(qb, kb) * SCALE * gam
                r = jnp.concatenate([v * beta, k * (beta * e_dc)], axis=-1)
                for s in range(6):
                    mb = m.astype(BF16)
                    if s < 5:
                        mr = _dot(mb, jnp.concatenate([r.astype(BF16), mb], axis=-1))
                        r = r + mr[:, 0:2 * HEAD_DIM]
                        m = mr[:, 2 * HEAD_DIM:3 * HEAD_DIM]
                    else:
                        r = r + _dot(mb, r.astype(BF16))
                u = r[:, 0:HEAD_DIM]
                w = r[:, HEAD_DIM:2 * HEAD_DIM]
                s_old = st[dirn, h]
                sb = s_old.astype(BF16)
                v_new = u - _dot(w.astype(BF16), sb)
                o = _dot(qb, sb) * (SCALE * e_dc) + _dot(qk.astype(BF16), v_new.astype(BF16))
                dl = dr[:, CHUNK - 1:CHUNK] if dirn == 0 else dr[:, 0:1]
                st[dirn, h] = s_old * jnp.exp(dl) + _dot_tn(kb, (v_new * jnp.exp(dl - dc)).astype(BF16))
                o_ref[0, pl.ds(r0, CHUNK), h * HEAD_DIM:(h + 1) * HEAD_DIM] = o
        return carry

    lax.fori_loop(0, nc, chunk_step, 0)


def _delta(proj, gt, cw, acol, dtcol, arow, dtrow):
    b, t, _ = proj.shape
    nb = t // TB
    hb = TB // SUBLANES
    last_h = t // SUBLANES - 1
    ba_blk = P_BA // LANES

    def specs(blk_of):
        return [
            pl.BlockSpec((1, TB, QKV_W), lambda bi, i: (bi, blk_of(i), 0)),
            pl.BlockSpec((1, SUBLANES, QKV_W), lambda bi, i: (bi, jnp.maximum(blk_of(i) * hb - 1, 0), 0)),
            pl.BlockSpec((1, SUBLANES, QKV_W), lambda bi, i: (bi, jnp.minimum((blk_of(i) + 1) * hb, last_h), 0)),
            pl.BlockSpec((1, TB, LANES), lambda bi, i: (bi, blk_of(i), ba_blk)),
            pl.BlockSpec((1, BA_ROWS, TB), lambda bi, i: (bi, 0, blk_of(i))),
        ]

    fwd = lambda i: i
    bwd = lambda i: nb - 1 - i
    const = lambda shape: pl.BlockSpec(shape, lambda bi, i: (0, 0))
    return pl.pallas_call(
        functools.partial(_delta_body, nb),
        grid=(b, nb),
        in_specs=specs(fwd) + specs(bwd) + [
            const((SUBLANES, QKV_W)), const((1, LANES)), const((1, LANES)), const((BA_ROWS, 1)), const((BA_ROWS, 1))],
        out_specs=[
            pl.BlockSpec((1, TB, A_WIDTH), lambda bi, i: (bi, i, 0)),
            pl.BlockSpec((1, TB, A_WIDTH), lambda bi, i: (bi, nb - 1 - i, 0)),
        ],
        out_shape=[jax.ShapeDtypeStruct((b, t, A_WIDTH), F32)] * 2,
        scratch_shapes=[
            pltpu.VMEM((2, TB + 2 * SUBLANES, QKV_W), F32),
            pltpu.VMEM((2, TB, QKV_W), F32),
            pltpu.VMEM((2, TB, LANES), F32),
            pltpu.VMEM((2, TB, LANES), F32),
            pltpu.VMEM((2, TB // CHUNK, BA_ROWS, CHUNK), F32),
            pltpu.VMEM((2, A_HEADS, HEAD_DIM, HEAD_DIM), F32),
        ],
        compiler_params=_params(("parallel", "arbitrary")),
        name="delta",
    )(proj, proj, proj, proj, gt, proj, proj, proj, proj, gt, cw, acol, dtcol, arow, dtrow)


def _retention_body(nb, qf, kf, vf, qb_, kb_, vb_, z_ref, of_ref, ob_ref, dmat, st):
    i = pl.program_id(1)
    nc = TB // RCHUNK

    @pl.when(i == 0)
    def _():
        st[...] = jnp.zeros_like(st)

    lg = -_softplus(-z_ref[...])
    ri = lax.broadcasted_iota(jnp.int32, (RCHUNK, RCHUNK), 0)
    ci = lax.broadcasted_iota(jnp.int32, (RCHUNK, RCHUNK), 1)
    diff = (ri - ci).astype(F32)
    pos = lax.broadcasted_iota(jnp.int32, (RCHUNK, 1), 0).astype(F32)
    lgs = []
    for h in range(B_HEADS):
        lgf = lg[0:1, h:h + 1]
        lgb = lg[1:2, h:h + 1]
        lgs.append((lgf, lgb))
        dmat[h] = SCALE * (jnp.where(ri >= ci, jnp.exp(diff * lgf), 0.0) + jnp.where(ci >= ri, jnp.exp(-diff * lgb), 0.0))

    def chunk_step(c, carry):
        rf = pl.multiple_of(c * RCHUNK, RCHUNK)
        rb = pl.multiple_of((nc - 1 - c) * RCHUNK, RCHUNK)
        for h in range(B_HEADS):
            lgf, lgb = lgs[h]
            cols = slice(h * HEAD_DIM, (h + 1) * HEAD_DIM)
            q = qf[0, pl.ds(rf, RCHUNK), cols].astype(BF16)
            k = kf[0, pl.ds(rf, RCHUNK), cols].astype(BF16)
            v = vf[0, pl.ds(rf, RCHUNK), cols]
            s_old = st[0, h]
            o = _dot((_dot_nt(q, k) * dmat[h]).astype(BF16), v.astype(BF16))
            o = o + _dot(q, s_old.astype(BF16)) * (SCALE * jnp.exp((pos + 1.0) * lgf))
            st[0, h] = s_old * jnp.exp(RCHUNK * lgf) + _dot_tn(k, (v * jnp.exp((RCHUNK - 1.0 - pos) * lgf)).astype(BF16))
            of_ref[0, pl.ds(rf, RCHUNK), cols] = o
            q = qb_[0, pl.ds(rb, RCHUNK), cols].astype(BF16)
            k = kb_[0, pl.ds(rb, RCHUNK), cols].astype(BF16)
            v = vb_[0, pl.ds(rb, RCHUNK), cols]
            s_old = st[1, h]
            ob_ref[0, pl.ds(rb, RCHUNK), cols] = _dot(q, s_old.astype(BF16)) * (SCALE * jnp.exp((RCHUNK - pos) * lgb))
            st[1, h] = s_old * jnp.exp(RCHUNK * lgb) + _dot_tn(k, (v * jnp.exp(pos * lgb)).astype(BF16))
        return carry

    lax.fori_loop(0, nc, chunk_step, 0)


def _retention(proj, z):
    b, t, _ = proj.shape
    nb = t // TB

    def spec(col, blk_of):
        return pl.BlockSpec((1, TB, B_WIDTH), lambda bi, i: (bi, blk_of(i), col // B_WIDTH))

    fwd = lambda i: i
    bwd = lambda i: nb - 1 - i
    return pl.pallas_call(
        functools.partial(_retention_body, nb),
        grid=(b, nb),
        in_specs=[spec(P_BQ, fwd), spec(P_BK, fwd), spec(P_BV, fwd), spec(P_BQ, bwd), spec(P_BK, bwd), spec(P_BV, bwd),
                  pl.BlockSpec((SUBLANES, LANES), lambda bi, i: (0, 0))],
        out_specs=[
            pl.BlockSpec((1, TB, B_WIDTH), lambda bi, i: (bi, i, 0)),
            pl.BlockSpec((1, TB, B_WIDTH), lambda bi, i: (bi, nb - 1 - i, 0)),
        ],
        out_shape=[jax.ShapeDtypeStruct((b, t, B_WIDTH), F32)] * 2,
        scratch_shapes=[
            pltpu.VMEM((B_HEADS, RCHUNK, RCHUNK), F32),
            pltpu.VMEM((2, B_HEADS, HEAD_DIM, HEAD_DIM), F32),
        ],
        compiler_params=_params(("parallel", "arbitrary")),
        name="retention",
    )(proj, proj, proj, proj, proj, proj, z)


def _attention_body(nq, q_ref, km, kp, kn, vm, vp, vn, gate_ref, sink_ref, o_ref, kx, vx):
    i = pl.program_id(1)
    nqb = TB // QBLOCK
    kx[0:QBLOCK, :] = kp[0].astype(BF16)
    kx[QBLOCK:QBLOCK + TB, :] = km[0].astype(BF16)
    kx[QBLOCK + TB:2 * QBLOCK + TB, :] = kn[0].astype(BF16)
    vx[0:QBLOCK, :] = vp[0].astype(BF16)
    vx[QBLOCK:QBLOCK + TB, :] = vm[0].astype(BF16)
    vx[QBLOCK + TB:2 * QBLOCK + TB, :] = vn[0].astype(BF16)

    qi = lax.broadcasted_iota(jnp.int32, (QBLOCK, 3 * QBLOCK), 0)
    kj = lax.broadcasted_iota(jnp.int32, (QBLOCK, 3 * QBLOCK), 1)
    rel = jnp.abs(kj - QBLOCK - qi)
    band = rel <= WINDOW
    relf = rel.astype(F32)
    first_lo = jnp.where(i == 0, QBLOCK, 0)
    last_hi = jnp.where(i == nq - 1, 2 * QBLOCK, 3 * QBLOCK)
    sink_all = sink_ref[...]
    for qb in range(nqb):
        rows = slice(qb * QBLOCK, (qb + 1) * QBLOCK)
        mask = band
        if qb == 0:
            mask = mask & (kj >= first_lo)
        if qb == nqb - 1:
            mask = mask & (kj < last_hi)
        keys = kx[qb * QBLOCK:(qb + 3) * QBLOCK, :]
        vals = vx[qb * QBLOCK:(qb + 3) * QBLOCK, :]
        for g in range(C_KV_HEADS):
            kg = keys[:, g * HEAD_DIM:(g + 1) * HEAD_DIM]
            vg = vals[:, g * HEAD_DIM:(g + 1) * HEAD_DIM]
            for j in range(C_GROUP):
                hq = g * C_GROUP + j
                cols = slice(hq * HEAD_DIM, (hq + 1) * HEAD_DIM)
                slope = 2.0 ** (-8.0 * (hq + 1) / C_HEADS)
                qh = q_ref[0, rows, cols].astype(BF16)
                s = _dot_nt(qh, kg) * SCALE - slope * relf
                s = jnp.where(mask, s, -jnp.inf)
                sink = sink_all[:, hq:hq + 1]
                m = jnp.maximum(jnp.max(s, axis=-1, keepdims=True), sink)
                p = jnp.exp(s - m)
                den = jnp.sum(p, axis=-1, keepdims=True) + jnp.exp(sink - m)
                o = _dot(p.astype(BF16), vg) / den
                o_ref[0, rows, cols] = o * _silu(gate_ref[0, rows, cols])


def _attention(proj, sink):
    b, t, _ = proj.shape
    nq = t // TB
    per = TB // QBLOCK
    last = t // QBLOCK - 1

    def halo(col):
        cb = col // C_KV_WIDTH
        return [
            pl.BlockSpec((1, TB, C_KV_WIDTH), lambda bi, i: (bi, i, cb)),
            pl.BlockSpec((1, QBLOCK, C_KV_WIDTH), lambda bi, i: (bi, jnp.maximum(i * per - 1, 0), cb)),
            pl.BlockSpec((1, QBLOCK, C_KV_WIDTH), lambda bi, i: (bi, jnp.minimum((i + 1) * per, last), cb)),
        ]

    return pl.pallas_call(
        functools.partial(_attention_body, nq),
        grid=(b, nq),
        in_specs=[pl.BlockSpec((1, TB, C_WIDTH), lambda bi, i: (bi, i, P_CQ // C_WIDTH))] + halo(P_CK) + halo(P_CV) + [
            pl.BlockSpec((1, TB, C_WIDTH), lambda bi, i: (bi, i, P_CG // C_WIDTH)),
            pl.BlockSpec((1, LANES), lambda bi, i: (0, 0))],
        out_specs=pl.BlockSpec((1, TB, C_WIDTH), lambda bi, i: (bi, i, 0)),
        out_shape=jax.ShapeDtypeStruct((b, t, C_WIDTH), F32),
        scratch_shapes=[pltpu.VMEM((TB + 2 * QBLOCK, C_KV_WIDTH), BF16), pltpu.VMEM((TB + 2 * QBLOCK, C_KV_WIDTH), BF16)],
        compiler_params=_params(("parallel", "parallel")),
        name="attention",
    )(proj, proj, proj, proj, proj, proj, proj, proj, sink)


def _out_proj_body(final, h_ref, af, ab, ag, bf, bb, bg, mc, p_ref, dng, rng, wo, wple, wpg, fg, o_ref):
    lo_lane = _pair_masks(TB)
    inv = 1.0 / HEAD_DIM

    def head_stat(x):
        outs = []
        for p in range(x.shape[-1] // LANES):
            xb = x[:, p * LANES:(p + 1) * LANES]
            s_lo = jnp.sum(jnp.where(lo_lane, xb, 0.0), axis=-1, keepdims=True)
            s_hi = jnp.sum(jnp.where(lo_lane, 0.0, xb), axis=-1, keepdims=True)
            outs.append(jnp.where(lo_lane, s_lo, s_hi) * inv)
        return jnp.concatenate(outs, axis=-1)

    oa = af[0] + ab[0]
    ya = oa * lax.rsqrt(head_stat(oa * oa) + EPS) * dng[...] * _silu(ag[0])
    ob = bf[0] + bb[0]
    xc = ob - head_stat(ob)
    yb = xc * lax.rsqrt(head_stat(xc * xc) + EPS) * rng[...] * _silu(bg[0])
    mix = jnp.concatenate([ya, yb, mc[0]], axis=-1).astype(BF16)
    h1 = h_ref[0] + _dot(mix, wo[...])
    h2 = h1 + _dot(p_ref[0, 0].astype(BF16), wple[...]) * _sigmoid(_dot(h1.astype(BF16), wpg[...]))
    if final:
        ms = jnp.mean(h2 * h2, axis=-1, keepdims=True)
        h2 = h2 * lax.rsqrt(ms + EPS) * fg[...]
    o_ref[0] = h2


def _out_proj(final, layer, h, oaf, oab, obf, obb, mixc, proj, p, dng, rng, wo, wple, wpg, fg):
    b, t, _ = h.shape
    row = lambda w: pl.BlockSpec((1, TB, w), lambda bi, i: (bi, i, 0))
    col = lambda w, c: pl.BlockSpec((1, TB, w), lambda bi, i: (bi, i, c // w))
    const = lambda shape: pl.BlockSpec(shape, lambda bi, i: (0, 0))
    return pl.pallas_call(
        functools.partial(_out_proj_body, final),
        grid=(b, t // TB),
        in_specs=[row(D_MODEL), row(A_WIDTH), row(A_WIDTH), col(A_WIDTH, P_AG), row(B_WIDTH), row(B_WIDTH), col(B_WIDTH, P_BG),
                  row(C_WIDTH), pl.BlockSpec((1, 1, TB, PLE_DIM), lambda bi, i: (layer, bi, i, 0)),
                  const((1, A_WIDTH)), const((1, B_WIDTH)), const((D_MODEL, D_MODEL)), const((PLE_DIM, D_MODEL)),
                  const((D_MODEL, D_MODEL)), const((1, D_MODEL))],
        out_specs=row(D_MODEL),
        out_shape=jax.ShapeDtypeStruct((b, t, D_MODEL), F32),
        compiler_params=_params(("parallel", "parallel")),
        name="out_proj",
    )(h, oaf, oab, proj, obf, obb, proj, mixc, p, dng, rng, wo, wple, wpg, fg)


def _layer_weights(i, w_in, w_out, norm_g, conv_w, dn_a_log, dn_dt_bias, dn_norm_g, ret_decay_z, ret_norm_g, attn_sink,
                   w_ple, w_pg):
    w = w_in[i]
    w_p = jnp.concatenate([w[:, :ORIG_BA], w[:, ORIG_BA + N_BA:], w[:, ORIG_BA:ORIG_BA + N_BA],
                           jnp.zeros((D_MODEL, LANES - N_BA), F32)], axis=-1).astype(BF16)
    w_ba_t = jnp.pad(w[:, ORIG_BA:ORIG_BA + N_BA].T, ((0, BA_ROWS - N_BA), (0, 0))).astype(BF16)
    a = jnp.exp(dn_a_log[i].astype(F32)).reshape(-1)
    dt = dn_dt_bias[i].astype(F32).reshape(-1)
    place = lambda x, n: jnp.zeros((n,), F32).at[G_OFF:G_OFF + 2 * A_HEADS].set(x)
    return dict(
        norm_g=norm_g[i].reshape(1, D_MODEL).astype(F32), w_p=w_p, w_ba_t=w_ba_t,
        cw=jnp.pad(conv_w[i].astype(F32), ((0, SUBLANES - CONV_K), (0, 0))),
        acol=place(a, LANES).reshape(1, LANES), dtcol=place(dt, LANES).reshape(1, LANES),
        arow=place(a, BA_ROWS).reshape(BA_ROWS, 1), dtrow=place(dt, BA_ROWS).reshape(BA_ROWS, 1),
        z=jnp.pad(ret_decay_z[i].astype(F32), ((0, SUBLANES - 2), (0, LANES - B_HEADS))),
        sink=jnp.pad(attn_sink[i].astype(F32), (0, LANES - C_HEADS)).reshape(1, LANES),
        dng=jnp.tile(dn_norm_g[i].astype(F32), A_HEADS).reshape(1, A_WIDTH),
        rng=ret_norm_g[i].astype(F32).reshape(1, B_WIDTH),
        wo=w_out[i].astype(BF16), wple=w_ple[i].astype(BF16), wpg=w_pg[i].astype(BF16),
    )


def _trunk(x, p, layers, final_g):
    h = x
    fg = final_g.reshape(1, D_MODEL).astype(F32)
    for i, lw in enumerate(layers):
        proj, gt = _in_proj(h, lw["norm_g"], lw["w_p"], lw["w_ba_t"])
        oaf, oab = _delta(proj, gt, lw["cw"], lw["acol"], lw["dtcol"], lw["arow"], lw["dtrow"])
        obf, obb = _retention(proj, lw["z"])
        mixc = _attention(proj, lw["sink"])
        h = _out_proj(i == len(layers) - 1, i, h, oaf, oab, obf, obb, mixc, proj, p, lw["dng"], lw["rng"],
                      lw["wo"], lw["wple"], lw["wpg"], fg)
    return h


def kernel(x_prompt, x_sample, p_prompt, p_sample, w_in, w_out, norm_g, conv_w, dn_a_log, dn_dt_bias, dn_norm_g,
           ret_decay_z, ret_norm_g, attn_sink, w_ple, w_pg, final_g):
    depth = w_in.shape[0]
    layers = [_layer_weights(i, w_in, w_out, norm_g, conv_w, dn_a_log, dn_dt_bias, dn_norm_g, ret_decay_z, ret_norm_g,
                             attn_sink, w_ple, w_pg) for i in range(depth)]
    y_prompt = _trunk(x_prompt, p_prompt, layers, final_g)
    y_sample = _trunk(x_sample, p_sample, layers, final_g)
    return (y_prompt, y_sample)
```

```python
import functools

import jax
import jax.numpy as jnp
from jax import lax
from jax.experimental import pallas as pl
from jax.experimental.pallas import tpu as pltpu

F32 = jnp.float32
BF16 = jnp.bfloat16

D_MODEL = 1024
HEAD_DIM = 64
A_HEADS = 6
A_WIDTH = A_HEADS * HEAD_DIM
B_HEADS = 6
B_WIDTH = B_HEADS * HEAD_DIM
C_HEADS = 4
C_KV_HEADS = 2
C_GROUP = C_HEADS // C_KV_HEADS
C_WIDTH = C_HEADS * HEAD_DIM
C_KV_WIDTH = C_KV_HEADS * HEAD_DIM
CONV_K = 5
WINDOW = 128
PLE_DIM = 256
EPS = 1e-6
SCALE = HEAD_DIM ** -0.5

LANES = 128
SUBLANES = 8
VMEM_LIMIT = 48 * 1024 * 1024

QKV_W = 3 * A_WIDTH
P_AG = QKV_W
P_BQ = P_AG + A_WIDTH
P_BK = P_BQ + B_WIDTH
P_BV = P_BK + B_WIDTH
P_BG = P_BV + B_WIDTH
P_CQ = P_BG + B_WIDTH
P_CK = P_CQ + C_WIDTH
P_CV = P_CK + C_KV_WIDTH
P_CG = P_CV + C_KV_WIDTH
P_BA = P_CG + C_WIDTH
PROJ_W = P_BA + LANES
ORIG_BA = QKV_W + A_WIDTH
N_BA = 4 * A_HEADS
BA_ROWS = 32
G_OFF = 2 * A_HEADS

CHUNK = 64
RCHUNK = 128
QBLOCK = 128
TB = 512
DELTA_UNROLL = 1

NT_DIMS = (((1,), (1,)), ((), ()))
TN_DIMS = (((0,), (0,)), ((), ()))


def _dot(a, b):
    return jnp.dot(a, b, preferred_element_type=F32)


def _dot_nt(a, b):
    return lax.dot_general(a, b, NT_DIMS, preferred_element_type=F32)


def _dot_tn(a, b):
    return lax.dot_general(a, b, TN_DIMS, preferred_element_type=F32)


def _sigmoid(x):
    return 1.0 / (1.0 + jnp.exp(-x))


def _silu(x):
    return x * _sigmoid(x)


def _softplus(x):
    return jnp.maximum(x, 0.0) + jnp.log1p(jnp.exp(-jnp.abs(x)))


def _split3(x):
    hi = x.astype(BF16)
    r1 = x - hi.astype(F32)
    mid = r1.astype(BF16)
    lo = (r1 - mid.astype(F32)).astype(BF16)
    return hi, mid, lo


def _pair_masks(rows):
    lane = lax.broadcasted_iota(jnp.int32, (rows, LANES), 1)
    return lane < HEAD_DIM


def _params(sem):
    return pltpu.CompilerParams(dimension_semantics=sem, vmem_limit_bytes=VMEM_LIMIT)


def _in_proj_body(h_ref, g_ref, w_ref, wba_ref, proj_ref, gt_ref):
    x = h_ref[0]
    ms = jnp.mean(x * x, axis=-1, keepdims=True)
    xn = (x * lax.rsqrt(ms + EPS) * g_ref[...]).astype(BF16)
    for c0 in range(0, PROJ_W, 512):
        c1 = min(c0 + 512, PROJ_W)
        proj_ref[0, :, c0:c1] = _dot(xn, w_ref[:, c0:c1])
    gt_ref[0] = _dot_nt(wba_ref[...], xn)


def _in_proj(h, norm_g, w_p, w_ba_t):
    b, t, _ = h.shape
    return pl.pallas_call(
        _in_proj_body,
        grid=(b, t // TB),
        in_specs=[
            pl.BlockSpec((1, TB, D_MODEL), lambda bi, i: (bi, i, 0)),
            pl.BlockSpec((1, D_MODEL), lambda bi, i: (0, 0)),
            pl.BlockSpec((D_MODEL, PROJ_W), lambda bi, i: (0, 0)),
            pl.BlockSpec((BA_ROWS, D_MODEL), lambda bi, i: (0, 0)),
        ],
        out_specs=[
            pl.BlockSpec((1, TB, PROJ_W), lambda bi, i: (bi, i, 0)),
            pl.BlockSpec((1, BA_ROWS, TB), lambda bi, i: (bi, 0, i)),
        ],
        out_shape=[jax.ShapeDtypeStruct((b, t, PROJ_W), F32), jax.ShapeDtypeStruct((b, BA_ROWS, t), F32)],
        compiler_params=_params(("parallel", "parallel")),
        name="in_proj",
    )(h, norm_g, w_p, w_ba_t)


def _delta_body(nb, *refs):
    (main_f, prev_f, next_f, ba_f, gt_f, main_b, prev_b, next_b, ba_b, gt_b,
     cw_ref, acol_ref, dtcol_ref, arow_ref, dtrow_ref,
     of_ref, ob_ref, xs, ys, bcol, dcol, drow, st) = refs
    i = pl.program_id(1)
    nc = TB // CHUNK

    @pl.when(i == 0)
    def _():
        st[...] = jnp.zeros_like(st)

    ri = lax.broadcasted_iota(jnp.int32, (CHUNK, CHUNK), 0)
    ci = lax.broadcasted_iota(jnp.int32, (CHUNK, CHUNK), 1)
    incl = (ri >= ci, ri <= ci)
    strict = (ri > ci, ri < ci)
    lo_lane = _pair_masks(CHUNK)
    cw = cw_ref[...]

    ins = ((main_f, prev_f, next_f, ba_f, gt_f, of_ref), (main_b, prev_b, next_b, ba_b, gt_b, ob_ref))
    for dirn, (main, prev, nxt, ba, gt, _) in enumerate(ins):
        blk = i if dirn == 0 else nb - 1 - i
        xs[dirn, 0:SUBLANES, :] = jnp.where(blk > 0, prev[0], 0.0)
        xs[dirn, SUBLANES:TB + SUBLANES, :] = main[0]
        xs[dirn, TB + SUBLANES:TB + 2 * SUBLANES, :] = jnp.where(blk < nb - 1, nxt[0], 0.0)
        cum = incl[dirn].astype(BF16)
        cum_t = incl[1 - dirn].astype(BF16)
        for c in range(nc):
            r0 = c * CHUNK
            base = r0 + SUBLANES - CONV_K // 2
            acc = cw[0:1, :] * xs[dirn, base:base + CHUNK, :]
            for k in range(1, CONV_K):
                acc = acc + cw[k:k + 1, :] * xs[dirn, base + k:base + k + CHUNK, :]
            y = _silu(acc)
            for p in range(2 * A_WIDTH // LANES):
                blk128 = y[:, p * LANES:(p + 1) * LANES]
                sq = blk128 * blk128
                s_lo = jnp.sum(jnp.where(lo_lane, sq, 0.0), axis=-1, keepdims=True)
                s_hi = jnp.sum(jnp.where(lo_lane, 0.0, sq), axis=-1, keepdims=True)
                r = jnp.where(lo_lane, lax.rsqrt(s_lo + EPS), lax.rsqrt(s_hi + EPS))
                ys[dirn, r0:r0 + CHUNK, p * LANES:(p + 1) * LANES] = blk128 * r
            ys[dirn, r0:r0 + CHUNK, 2 * A_WIDTH:QKV_W] = y[:, 2 * A_WIDTH:QKV_W]
            bac = ba[0, r0:r0 + CHUNK, :]
            bcol[dirn, r0:r0 + CHUNK, :] = _sigmoid(bac)
            g_c = -acol_ref[...] * _softplus(bac + dtcol_ref[...])
            h3, m3, l3 = _split3(g_c)
            d3 = _dot(cum, jnp.concatenate([h3, m3, l3], axis=-1))
            dcol[dirn, r0:r0 + CHUNK, :] = (d3[:, 0:LANES] + d3[:, LANES:2 * LANES]) + d3[:, 2 * LANES:3 * LANES]
            gtc = gt[0, :, r0:r0 + CHUNK]
            g_r = -arow_ref[...] * _softplus(gtc + dtrow_ref[...])
            h3, m3, l3 = _split3(g_r)
            d3 = _dot(jnp.concatenate([h3, m3, l3], axis=0), cum_t)
            drow[dirn, c] = (d3[0:BA_ROWS] + d3[BA_ROWS:2 * BA_ROWS]) + d3[2 * BA_ROWS:3 * BA_ROWS]

    def chunk_group(cg, carry):
        chains = []
        for j in range(DELTA_UNROLL):
            c = cg * DELTA_UNROLL + j
            for dirn in range(2):
                cc = c if dirn == 0 else nc - 1 - c
                r0 = pl.multiple_of(cc * CHUNK, CHUNK)
                beta_all = bcol[dirn, pl.ds(r0, CHUNK), :]
                dcol_all = dcol[dirn, pl.ds(r0, CHUNK), :]
                drow_all = drow[dirn, cc]
                for h in range(A_HEADS):
                    lb = A_HEADS * dirn + h
                    lg = G_OFF + lb
                    dc = dcol_all[:, lg:lg + 1]
                    dr = drow_all[lg:lg + 1, :]
                    k = ys[dirn, pl.ds(r0, CHUNK), A_WIDTH + h * HEAD_DIM:A_WIDTH + (h + 1) * HEAD_DIM]
                    chains.append(dict(
                        j=j, dirn=dirn, h=h, r0=r0, dc=dc, dr=dr, k=k, beta=beta_all[:, lb:lb + 1],
                        kb=k.astype(BF16),
                        qb=ys[dirn, pl.ds(r0, CHUNK), h * HEAD_DIM:(h + 1) * HEAD_DIM].astype(BF16),
                        v=ys[dirn, pl.ds(r0, CHUNK), 2 * A_WIDTH + h * HEAD_DIM:2 * A_WIDTH + (h + 1) * HEAD_DIM]))
        gps = [_dot_nt(jnp.concatenate([x["kb"], x["qb"]], axis=0), x["kb"]) for x in chains]
        ms, rs = [], []
        for x, gp in zip(chains, gps):
            gam = jnp.where(incl[x["dirn"]], jnp.exp(x["dc"] - x["dr"]), 0.0)
            x["e_dc"] = jnp.exp(x["dc"])
            ms.append(-jnp.where(strict[x["dirn"]], gp[0:CHUNK] * x["beta"] * gam, 0.0))
            x["qk"] = (gp[CHUNK:2 * CHUNK] * SCALE * gam).astype(BF16)
            rs.append(jnp.concatenate([x["v"] * x["beta"], x["k"] * (x["beta"] * x["e_dc"])], axis=-1))
        for s in range(6):
            mbs = [m.astype(BF16) for m in ms]
            if s < 5:
                mrs = [_dot(mb, jnp.concatenate([r.astype(BF16), mb], axis=-1)) for mb, r in zip(mbs, rs)]
                rs = [r + mr[:, 0:2 * HEAD_DIM] for r, mr in zip(rs, mrs)]
                ms = [mr[:, 2 * HEAD_DIM:3 * HEAD_DIM] for mr in mrs]
            else:
                rs = [r + _dot(mb, r.astype(BF16)) for mb, r in zip(mbs, rs)]
        for j in range(DELTA_UNROLL):
            sel = [(x, r) for x, r in zip(chains, rs) if x["j"] == j]
            olds = [st[x["dirn"], x["h"]] for x, _ in sel]
            wqs = [_dot(jnp.concatenate([r[:, HEAD_DIM:2 * HEAD_DIM].astype(BF16), x["qb"]], axis=0), s_old.astype(BF16))
                   for (x, r), s_old in zip(sel, olds)]
            vns = [r[:, 0:HEAD_DIM] - wq[0:CHUNK] for (_, r), wq in zip(sel, wqs)]
            o2s = [_dot(x["qk"], vn.astype(BF16)) for (x, _), vn in zip(sel, vns)]
            dls = [x["dr"][:, CHUNK - 1:CHUNK] if x["dirn"] == 0 else x["dr"][:, 0:1] for x, _ in sel]
            tns = [_dot_tn(x["kb"], (vn * jnp.exp(dl - x["dc"])).astype(BF16)) for (x, _), vn, dl in zip(sel, vns, dls)]
            for (x, _), s_old, wq, o2, dl, tn in zip(sel, olds, wqs, o2s, dls, tns):
                st[x["dirn"], x["h"]] = s_old * jnp.exp(dl) + tn
                o_ref = ins[x["dirn"]][5]
                o_ref[0, pl.ds(x["r0"], CHUNK), x["h"] * HEAD_DIM:(x["h"] + 1) * HEAD_DIM] = (
                    wq[CHUNK:2 * CHUNK] * (SCALE * x["e_dc"]) + o2)
        return carry

    lax.fori_loop(0, nc // DELTA_UNROLL, chunk_group, 0)


def _delta(proj, gt, cw, acol, dtcol, arow, dtrow):
    b, t, _ = proj.shape
    nb = t // TB
    hb = TB // SUBLANES
    last_h = t // SUBLANES - 1
    ba_blk = P_BA // LANES

    def specs(blk_of):
        return [
            pl.BlockSpec((1, TB, QKV_W), lambda bi, i: (bi, blk_of(i), 0)),
            pl.BlockSpec((1, SUBLANES, QKV_W), lambda bi, i: (bi, jnp.maximum(blk_of(i) * hb - 1, 0), 0)),
            pl.BlockSpec((1, SUBLANES, QKV_W), lambda bi, i: (bi, jnp.minimum((blk_of(i) + 1) * hb, last_h), 0)),
            pl.BlockSpec((1, TB, LANES), lambda bi, i: (bi, blk_of(i), ba_blk)),
            pl.BlockSpec((1, BA_ROWS, TB), lambda bi, i: (bi, 0, blk_of(i))),
        ]

    fwd = lambda i: i
    bwd = lambda i: nb - 1 - i
    const = lambda shape: pl.BlockSpec(shape, lambda bi, i: (0, 0))
    return pl.pallas_call(
        functools.partial(_delta_body, nb),
        grid=(b, nb),
        in_specs=specs(fwd) + specs(bwd) + [
            const((SUBLANES, QKV_W)), const((1, LANES)), const((1, LANES)), const((BA_ROWS, 1)), const((BA_ROWS, 1))],
        out_specs=[
            pl.BlockSpec((1, TB, A_WIDTH), lambda bi, i: (bi, i, 0)),
            pl.BlockSpec((1, TB, A_WIDTH), lambda bi, i: (bi, nb - 1 - i, 0)),
        ],
        out_shape=[jax.ShapeDtypeStruct((b, t, A_WIDTH), F32)] * 2,
        scratch_shapes=[
            pltpu.VMEM((2, TB + 2 * SUBLANES, QKV_W), F32),
            pltpu.VMEM((2, TB, QKV_W), F32),
            pltpu.VMEM((2, TB, LANES), F32),
            pltpu.VMEM((2, TB, LANES), F32),
            pltpu.VMEM((2, TB // CHUNK, BA_ROWS, CHUNK), F32),
            pltpu.VMEM((2, A_HEADS, HEAD_DIM, HEAD_DIM), F32),
        ],
        compiler_params=_params(("parallel", "arbitrary")),
        name="delta",
    )(proj, proj, proj, proj, gt, proj, proj, proj, proj, gt, cw, acol, dtcol, arow, dtrow)


def _retention_body(nb, qf, kf, vf, qb_, kb_, vb_, z_ref, of_ref, ob_ref, dmat, tab, st):
    i = pl.program_id(1)
    nc = TB // RCHUNK
    npair = B_HEADS // 2

    @pl.when(i == 0)
    def _():
        st[...] = jnp.zeros_like(st)

    lg = -_softplus(-z_ref[...])
    ri = lax.broadcasted_iota(jnp.int32, (RCHUNK, RCHUNK), 0)
    ci = lax.broadcasted_iota(jnp.int32, (RCHUNK, RCHUNK), 1)
    diff = (ri - ci).astype(F32)
    pos = ri.astype(F32)
    lo = ci < HEAD_DIM
    same_head = (ri < HEAD_DIM) == lo
    for h in range(B_HEADS):
        lgf = lg[0:1, h:h + 1]
        lgb = lg[1:2, h:h + 1]
        dmat[h] = SCALE * (jnp.where(ri >= ci, jnp.exp(diff * lgf), 0.0) + jnp.where(ci >= ri, jnp.exp(-diff * lgb), 0.0))
    cds = []
    for p in range(npair):
        lgf = jnp.where(lo, lg[0:1, 2 * p:2 * p + 1], lg[0:1, 2 * p + 1:2 * p + 2])
        lgb = jnp.where(lo, lg[1:2, 2 * p:2 * p + 1], lg[1:2, 2 * p + 1:2 * p + 2])
        tab[p, 0] = SCALE * jnp.exp((pos + 1.0) * lgf)
        tab[p, 1] = jnp.exp((RCHUNK - 1.0 - pos) * lgf)
        tab[p, 2] = SCALE * jnp.exp((RCHUNK - pos) * lgb)
        tab[p, 3] = jnp.exp(pos * lgb)
        cds.append((jnp.exp(RCHUNK * lgf[0:1, :]), jnp.exp(RCHUNK * lgb[0:1, :])))

    def chunk_step(c, carry):
        rf = pl.multiple_of(c * RCHUNK, RCHUNK)
        rb = pl.multiple_of((nc - 1 - c) * RCHUNK, RCHUNK)
        loads = []
        for p in range(npair):
            cols = slice(p * LANES, (p + 1) * LANES)
            loads.append((qf[0, pl.ds(rf, RCHUNK), cols].astype(BF16), kf[0, pl.ds(rf, RCHUNK), cols].astype(BF16),
                          vf[0, pl.ds(rf, RCHUNK), cols], qb_[0, pl.ds(rb, RCHUNK), cols].astype(BF16),
                          kb_[0, pl.ds(rb, RCHUNK), cols].astype(BF16), vb_[0, pl.ds(rb, RCHUNK), cols]))
        zero = jnp.zeros((RCHUNK, LANES), BF16)
        s_lo = [_dot_nt(x[0], jnp.where(lo, x[1], zero)) for x in loads]
        s_hi = [_dot_nt(x[0], jnp.where(lo, zero, x[1])) for x in loads]
        qsf = [_dot(x[0], st[0, p].astype(BF16)) for p, x in enumerate(loads)]
        qsb = [_dot(x[3], st[1, p].astype(BF16)) for p, x in enumerate(loads)]
        tnf = [_dot_tn(x[1], (x[2] * tab[p, 1]).astype(BF16)) for p, x in enumerate(loads)]
        tnb = [_dot_tn(x[4], (x[5] * tab[p, 3]).astype(BF16)) for p, x in enumerate(loads)]
        intra = []
        for p, x in enumerate(loads):
            vb16 = x[2].astype(BF16)
            pm = jnp.concatenate([(s_lo[p] * dmat[2 * p]).astype(BF16), (s_hi[p] * dmat[2 * p + 1]).astype(BF16)], axis=-1)
            v2 = jnp.concatenate([jnp.where(lo, vb16, zero), jnp.where(lo, zero, vb16)], axis=0)
            intra.append(_dot(pm, v2))
        for p in range(npair):
            cols = slice(p * LANES, (p + 1) * LANES)
            of_ref[0, pl.ds(rf, RCHUNK), cols] = intra[p] + qsf[p] * tab[p, 0]
            ob_ref[0, pl.ds(rb, RCHUNK), cols] = qsb[p] * tab[p, 2]
            st[0, p] = st[0, p] * cds[p][0] + jnp.where(same_head, tnf[p], 0.0)
            st[1, p] = st[1, p] * cds[p][1] + jnp.where(same_head, tnb[p], 0.0)
        return carry

    lax.fori_loop(0, nc, chunk_step, 0)


def _retention(proj, z):
    b, t, _ = proj.shape
    nb = t // TB

    def spec(col, blk_of):
        return pl.BlockSpec((1, TB, B_WIDTH), lambda bi, i: (bi, blk_of(i), col // B_WIDTH))

    fwd = lambda i: i
    bwd = lambda i: nb - 1 - i
    return pl.pallas_call(
        functools.partial(_retention_body, nb),
        grid=(b, nb),
        in_specs=[spec(P_BQ, fwd), spec(P_BK, fwd), spec(P_BV, fwd), spec(P_BQ, bwd), spec(P_BK, bwd), spec(P_BV, bwd),
                  pl.BlockSpec((SUBLANES, LANES), lambda bi, i: (0, 0))],
        out_specs=[
            pl.BlockSpec((1, TB, B_WIDTH), lambda bi, i: (bi, i, 0)),
            pl.BlockSpec((1, TB, B_WIDTH), lambda bi, i: (bi, nb - 1 - i, 0)),
        ],
        out_shape=[jax.ShapeDtypeStruct((b, t, B_WIDTH), F32)] * 2,
        scratch_shapes=[
            pltpu.VMEM((B_HEADS, RCHUNK, RCHUNK), F32),
            pltpu.VMEM((B_HEADS // 2, 4, RCHUNK, LANES), F32),
            pltpu.VMEM((2, B_HEADS // 2, LANES, LANES), F32),
        ],
        compiler_params=_params(("parallel", "arbitrary")),
        name="retention",
    )(proj, proj, proj, proj, proj, proj, z)


def _attention_body(nq, q_ref, km, kp, kn, vm, vp, vn, gate_ref, sink_ref, o_ref, kx, vx, bias):
    i = pl.program_id(1)
    nqb = TB // QBLOCK
    lo_k = lax.broadcasted_iota(jnp.int32, (TB + 2 * QBLOCK, LANES), 1) < HEAD_DIM
    for src, dst in (((kp, km, kn), kx), ((vp, vm, vn), vx)):
        full = jnp.concatenate([src[0][0], src[1][0], src[2][0]], axis=0)
        swapped = pltpu.roll(full, HEAD_DIM, 1)
        dst[0] = jnp.where(lo_k, full, swapped).astype(BF16)
        dst[1] = jnp.where(lo_k, swapped, full).astype(BF16)

    qi = lax.broadcasted_iota(jnp.int32, (QBLOCK, 3 * QBLOCK), 0)
    kj = lax.broadcasted_iota(jnp.int32, (QBLOCK, 3 * QBLOCK), 1)
    rel = jnp.abs(kj - QBLOCK - qi)
    relf = rel.astype(F32)
    for hq in range(C_HEADS):
        slope = 2.0 ** (-8.0 * (hq + 1) / C_HEADS)
        bias[hq] = jnp.where(rel <= WINDOW, -slope * relf, -jnp.inf)
    first_lo = jnp.where(i == 0, QBLOCK, 0)
    last_hi = jnp.where(i == nq - 1, 2 * QBLOCK, 3 * QBLOCK)
    sink_all = sink_ref[...]
    lo_q = lax.broadcasted_iota(jnp.int32, (QBLOCK, LANES), 1) < HEAD_DIM
    zero = jnp.zeros((QBLOCK, LANES), BF16)

    items = [(qb, g) for qb in range(nqb) for g in range(C_KV_HEADS)]

    def scores(item):
        qb, g = item
        q = (q_ref[0, qb * QBLOCK:(qb + 1) * QBLOCK, g * LANES:(g + 1) * LANES] * SCALE).astype(BF16)
        qm = jnp.concatenate([jnp.where(lo_q, q, zero), jnp.where(lo_q, zero, q)], axis=0)
        return _dot_nt(qm, kx[g, qb * QBLOCK:(qb + 3) * QBLOCK, :])

    s_next = scores(items[0])
    for n, (qb, g) in enumerate(items):
        s_all = s_next
        if n + 1 < len(items):
            s_next = scores(items[n + 1])
        ps, dens = [], []
        for j in range(C_GROUP):
            hq = g * C_GROUP + j
            s = s_all[j * QBLOCK:(j + 1) * QBLOCK] + bias[hq]
            if qb == 0:
                s = jnp.where(kj >= first_lo, s, -jnp.inf)
            if qb == nqb - 1:
                s = jnp.where(kj < last_hi, s, -jnp.inf)
            sink = sink_all[:, hq:hq + 1]
            m = jnp.maximum(jnp.max(s, axis=-1, keepdims=True), sink)
            p = jnp.exp(s - m)
            dens.append(jnp.sum(p, axis=-1, keepdims=True) + jnp.exp(sink - m))
            ps.append(p.astype(BF16))
        o2 = _dot(jnp.concatenate(ps, axis=0), vx[g, qb * QBLOCK:(qb + 3) * QBLOCK, :])
        o = jnp.where(lo_q, o2[0:QBLOCK] / dens[0], o2[QBLOCK:2 * QBLOCK] / dens[1])
        rows = slice(qb * QBLOCK, (qb + 1) * QBLOCK)
        cols = slice(g * LANES, (g + 1) * LANES)
        o_ref[0, rows, cols] = o * _silu(gate_ref[0, rows, cols])


def _attention(proj, sink):
    b, t, _ = proj.shape
    nq = t // TB
    per = TB // QBLOCK
    last = t // QBLOCK - 1

    def halo(col):
        cb = col // C_KV_WIDTH
        return [
            pl.BlockSpec((1, TB, C_KV_WIDTH), lambda bi, i: (bi, i, cb)),
            pl.BlockSpec((1, QBLOCK, C_KV_WIDTH), lambda bi, i: (bi, jnp.maximum(i * per - 1, 0), cb)),
            pl.BlockSpec((1, QBLOCK, C_KV_WIDTH), lambda bi, i: (bi, jnp.minimum((i + 1) * per, last), cb)),
        ]

    return pl.pallas_call(
        functools.partial(_attention_body, nq),
        grid=(b, nq),
        in_specs=[pl.BlockSpec((1, TB, C_WIDTH), lambda bi, i: (bi, i, P_CQ // C_WIDTH))] + halo(P_CK) + halo(P_CV) + [
            pl.BlockSpec((1, TB, C_WIDTH), lambda bi, i: (bi, i, P_CG // C_WIDTH)),
            pl.BlockSpec((1, LANES), lambda bi, i: (0, 0))],
        out_specs=pl.BlockSpec((1, TB, C_WIDTH), lambda bi, i: (bi, i, 0)),
        out_shape=jax.ShapeDtypeStruct((b, t, C_WIDTH), F32),
        scratch_shapes=[
            pltpu.VMEM((C_KV_HEADS, TB + 2 * QBLOCK, LANES), BF16),
            pltpu.VMEM((C_KV_HEADS, TB + 2 * QBLOCK, LANES), BF16),
            pltpu.VMEM((C_HEADS, QBLOCK, 3 * QBLOCK), F32),
        ],
        compiler_params=_params(("parallel", "parallel")),
        name="attention",
    )(proj, proj, proj, proj, proj, proj, proj, proj, sink)


def _out_proj_body(final, h_ref, af, ab, ag, bf, bb, bg, mc, p_ref, dng, rng, wo, wple, wpg, fg, o_ref):
    lo_lane = _pair_masks(TB)
    inv = 1.0 / HEAD_DIM

    def head_stat(x):
        outs = []
        for p in range(x.shape[-1] // LANES):
            xb = x[:, p * LANES:(p + 1) * LANES]
            s_lo = jnp.sum(jnp.where(lo_lane, xb, 0.0), axis=-1, keepdims=True)
            s_hi = jnp.sum(jnp.where(lo_lane, 0.0, xb), axis=-1, keepdims=True)
            outs.append(jnp.where(lo_lane, s_lo, s_hi) * inv)
        return jnp.concatenate(outs, axis=-1)

    oa = af[0] + ab[0]
    ya = oa * lax.rsqrt(head_stat(oa * oa) + EPS) * dng[...] * _silu(ag[0])
    ob = bf[0] + bb[0]
    xc = ob - head_stat(ob)
    yb = xc * lax.rsqrt(head_stat(xc * xc) + EPS) * rng[...] * _silu(bg[0])
    mix = jnp.concatenate([ya, yb, mc[0]], axis=-1).astype(BF16)
    h1 = h_ref[0] + _dot(mix, wo[...])
    h2 = h1 + _dot(p_ref[0, 0].astype(BF16), wple[...]) * _sigmoid(_dot(h1.astype(BF16), wpg[...]))
    if final:
        ms = jnp.mean(h2 * h2, axis=-1, keepdims=True)
        h2 = h2 * lax.rsqrt(ms + EPS) * fg[...]
    o_ref[0] = h2


def _out_proj(final, layer, h, oaf, oab, obf, obb, mixc, proj, p, dng, rng, wo, wple, wpg, fg):
    b, t, _ = h.shape
    row = lambda w: pl.BlockSpec((1, TB, w), lambda bi, i: (bi, i, 0))
    col = lambda w, c: pl.BlockSpec((1, TB, w), lambda bi, i: (bi, i, c // w))
    const = lambda shape: pl.BlockSpec(shape, lambda bi, i: (0, 0))
    return pl.pallas_call(
        functools.partial(_out_proj_body, final),
        grid=(b, t // TB),
        in_specs=[row(D_MODEL), row(A_WIDTH), row(A_WIDTH), col(A_WIDTH, P_AG), row(B_WIDTH), row(B_WIDTH), col(B_WIDTH, P_BG),
                  row(C_WIDTH), pl.BlockSpec((1, 1, TB, PLE_DIM), lambda bi, i: (layer, bi, i, 0)),
                  const((1, A_WIDTH)), const((1, B_WIDTH)), const((D_MODEL, D_MODEL)), const((PLE_DIM, D_MODEL)),
                  const((D_MODEL, D_MODEL)), const((1, D_MODEL))],
        out_specs=row(D_MODEL),
        out_shape=jax.ShapeDtypeStruct((b, t, D_MODEL), F32),
        compiler_params=_params(("parallel", "parallel")),
        name="out_proj",
    )(h, oaf, oab, proj, obf, obb, proj, mixc, p, dng, rng, wo, wple, wpg, fg)


def _layer_weights(i, w_in, w_out, norm_g, conv_w, dn_a_log, dn_dt_bias, dn_norm_g, ret_decay_z, ret_norm_g, attn_sink,
                   w_ple, w_pg):
    w = w_in[i]
    w_p = jnp.concatenate([w[:, :ORIG_BA], w[:, ORIG_BA + N_BA:], w[:, ORIG_BA:ORIG_BA + N_BA],
                           jnp.zeros((D_MODEL, LANES - N_BA), F32)], axis=-1).astype(BF16)
    w_ba_t = jnp.pad(w[:, ORIG_BA:ORIG_BA + N_BA].T, ((0, BA_ROWS - N_BA), (0, 0))).astype(BF16)
    a = jnp.exp(dn_a_log[i].astype(F32)).reshape(-1)
    dt = dn_dt_bias[i].astype(F32).reshape(-1)
    place = lambda x, n: jnp.zeros((n,), F32).at[G_OFF:G_OFF + 2 * A_HEADS].set(x)
    return dict(
        norm_g=norm_g[i].reshape(1, D_MODEL).astype(F32), w_p=w_p, w_ba_t=w_ba_t,
        cw=jnp.pad(conv_w[i].astype(F32), ((0, SUBLANES - CONV_K), (0, 0))),
        acol=place(a, LANES).reshape(1, LANES), dtcol=place(dt, LANES).reshape(1, LANES),
        arow=place(a, BA_ROWS).reshape(BA_ROWS, 1), dtrow=place(dt, BA_ROWS).reshape(BA_ROWS, 1),
        z=jnp.pad(ret_decay_z[i].astype(F32), ((0, SUBLANES - 2), (0, LANES - B_HEADS))),
        sink=jnp.pad(attn_sink[i].astype(F32), (0, LANES - C_HEADS)).reshape(1, LANES),
        dng=jnp.tile(dn_norm_g[i].astype(F32), A_HEADS).reshape(1, A_WIDTH),
        rng=ret_norm_g[i].astype(F32).reshape(1, B_WIDTH),
        wo=w_out[i].astype(BF16), wple=w_ple[i].astype(BF16), wpg=w_pg[i].astype(BF16),
    )


def _trunk(x, p, layers, final_g):
    h = x
    fg = final_g.reshape(1, D_MODEL).astype(F32)
    for i, lw in enumerate(layers):
        proj, gt = _in_proj(h, lw["norm_g"], lw["w_p"], lw["w_ba_t"])
        oaf, oab = _delta(proj, gt, lw["cw"], lw["acol"], lw["dtcol"], lw["arow"], lw["dtrow"])
        obf, obb = _retention(proj, lw["z"])
        mixc = _attention(proj, lw["sink"])
        h = _out_proj(i == len(layers) - 1, i, h, oaf, oab, obf, obb, mixc, proj, p, lw["dng"], lw["rng"],
                      lw["wo"], lw["wple"], lw["wpg"], fg)
    return h


def kernel(x_prompt, x_sample, p_prompt, p_sample, w_in, w_out, norm_g, conv_w, dn_a_log, dn_dt_bias, dn_norm_g,
           ret_decay_z, ret_norm_g, attn_sink, w_ple, w_pg, final_g):
    depth = w_in.shape[0]
    layers = [_layer_weights(i, w_in, w_out, norm_g, conv_w, dn_a_log, dn_dt_bias, dn_norm_g, ret_decay_z, ret_norm_g,
                             attn_sink, w_ple, w_pg) for i in range(depth)]
    y_prompt = _trunk(x_prompt, p_prompt, layers, final_g)
    y_sample = _trunk(x_sample, p_sample, layers, final_g)
    return (y_prompt, y_sample)
```

```python
import functools

import jax
import jax.numpy as jnp
from jax import lax
from jax.experimental import pallas as pl
from jax.experimental.pallas import tpu as pltpu

F32 = jnp.float32
BF16 = jnp.bfloat16

D_MODEL = 1024
HEAD_DIM = 64
A_HEADS = 6
A_WIDTH = A_HEADS * HEAD_DIM
B_HEADS = 6
B_WIDTH = B_HEADS * HEAD_DIM
C_HEADS = 4
C_KV_HEADS = 2
C_GROUP = C_HEADS // C_KV_HEADS
C_WIDTH = C_HEADS * HEAD_DIM
C_KV_WIDTH = C_KV_HEADS * HEAD_DIM
CONV_K = 5
WINDOW = 128
PLE_DIM = 256
EPS = 1e-6
SCALE = HEAD_DIM ** -0.5

LANES = 128
SUBLANES = 8
VMEM_LIMIT = 48 * 1024 * 1024

QKV_W = 3 * A_WIDTH
P_AG = QKV_W
P_BQ = P_AG + A_WIDTH
P_BK = P_BQ + B_WIDTH
P_BV = P_BK + B_WIDTH
P_BG = P_BV + B_WIDTH
P_CQ = P_BG + B_WIDTH
P_CK = P_CQ + C_WIDTH
P_CV = P_CK + C_KV_WIDTH
P_CG = P_CV + C_KV_WIDTH
P_BA = P_CG + C_WIDTH
PROJ_W = P_BA + LANES
ORIG_BA = QKV_W + A_WIDTH
N_BA = 4 * A_HEADS
BA_ROWS = 32
G_OFF = 2 * A_HEADS

CHUNK = 64
RCHUNK = 128
QBLOCK = 128
TB = 512

NT_DIMS = (((1,), (1,)), ((), ()))
TN_DIMS = (((0,), (0,)), ((), ()))


def _dot(a, b):
    return jnp.dot(a, b, preferred_element_type=F32)


def _dot_nt(a, b):
    return lax.dot_general(a, b, NT_DIMS, preferred_element_type=F32)


def _dot_tn(a, b):
    return lax.dot_general(a, b, TN_DIMS, preferred_element_type=F32)


def _sigmoid(x):
    return 0.5 * jnp.tanh(0.5 * x) + 0.5


def _silu(x):
    return x * _sigmoid(x)


def _softplus(x):
    return jnp.maximum(x, 0.0) + jnp.log1p(jnp.exp(-jnp.abs(x)))


def _split3(x):
    hi = x.astype(BF16)
    r1 = x - hi.astype(F32)
    mid = r1.astype(BF16)
    lo = (r1 - mid.astype(F32)).astype(BF16)
    return hi, mid, lo


def _pair_masks(rows):
    lane = lax.broadcasted_iota(jnp.int32, (rows, LANES), 1)
    return lane < HEAD_DIM


def _params(sem):
    return pltpu.CompilerParams(dimension_semantics=sem, vmem_limit_bytes=VMEM_LIMIT)


def _in_proj_body(nb, h_ref, hp_ref, hn_ref, g_ref, w_ref, wba_ref, cw_ref, proj_ref, gt_ref, xs):
    i = pl.program_id(1)
    x = jnp.concatenate([hp_ref[0], h_ref[0], hn_ref[0]], axis=0)
    ms = jnp.mean(x * x, axis=-1, keepdims=True)
    xn_ext = (x * lax.rsqrt(ms + EPS) * g_ref[...]).astype(BF16)
    xn = xn_ext[SUBLANES:TB + SUBLANES]
    row = lax.broadcasted_iota(jnp.int32, (TB + 2 * SUBLANES, 1), 0)
    inside = ((row >= SUBLANES) | (i > 0)) & ((row < TB + SUBLANES) | (i < nb - 1))
    lo_lane = _pair_masks(CHUNK)
    cw = cw_ref[...]
    base = SUBLANES - CONV_K // 2
    for part in range(3):
        c0 = part * A_WIDTH
        xs[part] = jnp.where(inside, _dot(xn_ext, w_ref[:, c0:c0 + A_WIDTH]), 0.0)
    for c0 in range(QKV_W, PROJ_W, 512):
        c1 = min(c0 + 512, PROJ_W)
        proj_ref[0, :, c0:c1] = _dot(xn, w_ref[:, c0:c1])
    gt_ref[0] = _dot_nt(wba_ref[...], xn)
    for part in range(3):
        c0 = part * A_WIDTH
        for r0 in range(0, TB, CHUNK):
            acc = cw[0:1, c0:c0 + A_WIDTH] * xs[part, base + r0:base + r0 + CHUNK, :]
            for k in range(1, CONV_K):
                acc = acc + cw[k:k + 1, c0:c0 + A_WIDTH] * xs[part, base + k + r0:base + k + r0 + CHUNK, :]
            y = _silu(acc)
            if part < 2:
                for p in range(A_WIDTH // LANES):
                    y2 = y[:, p * LANES:(p + 1) * LANES]
                    sq = y2 * y2
                    s_lo = jnp.sum(jnp.where(lo_lane, sq, 0.0), axis=-1, keepdims=True)
                    s_hi = jnp.sum(jnp.where(lo_lane, 0.0, sq), axis=-1, keepdims=True)
                    r = jnp.where(lo_lane, lax.rsqrt(s_lo + EPS), lax.rsqrt(s_hi + EPS))
                    proj_ref[0, r0:r0 + CHUNK, c0 + p * LANES:c0 + (p + 1) * LANES] = y2 * r
            else:
                proj_ref[0, r0:r0 + CHUNK, c0:c0 + A_WIDTH] = y


def _in_proj(h, norm_g, w_p, w_ba_t, cw):
    b, t, _ = h.shape
    nb = t // TB
    hb = TB // SUBLANES
    last_h = t // SUBLANES - 1
    const = lambda shape: pl.BlockSpec(shape, lambda bi, i: (0, 0))
    return pl.pallas_call(
        functools.partial(_in_proj_body, nb),
        grid=(b, nb),
        in_specs=[
            pl.BlockSpec((1, TB, D_MODEL), lambda bi, i: (bi, i, 0)),
            pl.BlockSpec((1, SUBLANES, D_MODEL), lambda bi, i: (bi, jnp.maximum(i * hb - 1, 0), 0)),
            pl.BlockSpec((1, SUBLANES, D_MODEL), lambda bi, i: (bi, jnp.minimum((i + 1) * hb, last_h), 0)),
            const((1, D_MODEL)), const((D_MODEL, PROJ_W)), const((BA_ROWS, D_MODEL)), const((SUBLANES, QKV_W)),
        ],
        out_specs=[
            pl.BlockSpec((1, TB, PROJ_W), lambda bi, i: (bi, i, 0)),
            pl.BlockSpec((1, BA_ROWS, TB), lambda bi, i: (bi, 0, i)),
        ],
        out_shape=[jax.ShapeDtypeStruct((b, t, PROJ_W), F32), jax.ShapeDtypeStruct((b, BA_ROWS, t), F32)],
        scratch_shapes=[pltpu.VMEM((3, TB + 2 * SUBLANES, A_WIDTH), F32)],
        compiler_params=_params(("parallel", "parallel")),
        name="in_proj",
    )(h, h, h, norm_g, w_p, w_ba_t, cw)


def _delta_body(nb, *refs):
    (qkv_f, ba_f, gt_f, qkv_b, ba_b, gt_b, acol_ref, dtcol_ref, arow_ref, dtrow_ref,
     of_ref, ob_ref, bcol, dcol, drow, st) = refs
    i = pl.program_id(1)
    nc = TB // CHUNK
    npair = A_HEADS // 2

    @pl.when(i == 0)
    def _():
        st[...] = jnp.zeros_like(st)

    ri = lax.broadcasted_iota(jnp.int32, (CHUNK, CHUNK), 0)
    ci = lax.broadcasted_iota(jnp.int32, (CHUNK, CHUNK), 1)
    incl = (ri >= ci, ri <= ci)

    ins = ((qkv_f, ba_f, gt_f, of_ref), (qkv_b, ba_b, gt_b, ob_ref))
    for dirn, (_, ba, gt, _) in enumerate(ins):
        cum = incl[dirn].astype(BF16)
        cum_t = incl[1 - dirn].astype(BF16)
        for c in range(nc):
            r0 = c * CHUNK
            bac = ba[0, r0:r0 + CHUNK, :]
            bcol[dirn, r0:r0 + CHUNK, :] = _sigmoid(bac)
            g_c = -acol_ref[...] * _softplus(bac + dtcol_ref[...])
            h3, m3, l3 = _split3(g_c)
            d3 = _dot(cum, jnp.concatenate([h3, m3, l3], axis=-1))
            dcol[dirn, r0:r0 + CHUNK, :] = (d3[:, 0:LANES] + d3[:, LANES:2 * LANES]) + d3[:, 2 * LANES:3 * LANES]
            gtc = gt[0, :, r0:r0 + CHUNK]
            g_r = -arow_ref[...] * _softplus(gtc + dtrow_ref[...])
            h3, m3, l3 = _split3(g_r)
            d3 = _dot(jnp.concatenate([h3, m3, l3], axis=0), cum_t)
            drow[dirn, c] = (d3[0:BA_ROWS] + d3[BA_ROWS:2 * BA_ROWS]) + d3[2 * BA_ROWS:3 * BA_ROWS]

    ri2 = lax.broadcasted_iota(jnp.int32, (CHUNK, LANES), 0)
    ln2 = lax.broadcasted_iota(jnp.int32, (CHUNK, LANES), 1)
    lo = ln2 < HEAD_DIM
    cj2 = jnp.where(lo, ln2, ln2 - HEAD_DIM)
    eye_lo = jnp.where(lo & (ri2 == cj2), 1.0, 0.0)
    keep = ((lo & (ri2 >= cj2)) | (~lo & (ri2 > cj2)),
            (lo & (ri2 <= cj2)) | (~lo & (ri2 < cj2)))
    rs2 = lax.broadcasted_iota(jnp.int32, (LANES, LANES), 0)
    cs2 = lax.broadcasted_iota(jnp.int32, (LANES, LANES), 1)
    same_head = (rs2 < HEAD_DIM) == (cs2 < HEAD_DIM)
    lo_row = lax.broadcasted_iota(jnp.int32, (1, LANES), 1) < HEAD_DIM
    zero_b = jnp.zeros((CHUNK, LANES), BF16)

    def chunk_step(c, carry):
        pairs, chains = [], []
        for dirn, (qkv, _, _, _) in enumerate(ins):
            cc = c if dirn == 0 else nc - 1 - c
            r0 = pl.multiple_of(cc * CHUNK, CHUNK)
            beta_all = bcol[dirn, pl.ds(r0, CHUNK), :]
            dcol_all = dcol[dirn, pl.ds(r0, CHUNK), :]
            drow_all = drow[dirn, cc]
            for p in range(npair):
                cols = lambda part: slice(part * A_WIDTH + p * LANES, part * A_WIDTH + (p + 1) * LANES)
                q2 = qkv[0, pl.ds(r0, CHUNK), cols(0)]
                k2 = qkv[0, pl.ds(r0, CHUNK), cols(1)]
                v2 = qkv[0, pl.ds(r0, CHUNK), cols(2)]
                pair = dict(dirn=dirn, p=p, r0=r0, k2=k2, v2=v2, qb=q2.astype(BF16), kb=k2.astype(BF16))
                pairs.append(pair)
                for e in range(2):
                    lb = A_HEADS * dirn + 2 * p + e
                    lg = G_OFF + lb
                    dr = drow_all[lg:lg + 1, :]
                    chains.append(dict(
                        pair=pair, e=e, dirn=dirn, own=lo if e == 0 else ~lo,
                        beta=jnp.broadcast_to(beta_all[:, lb:lb + 1], (CHUNK, LANES)),
                        dc=jnp.broadcast_to(dcol_all[:, lg:lg + 1], (CHUNK, LANES)),
                        dr2=jnp.concatenate([dr, dr], axis=-1),
                        dl=dr[:, CHUNK - 1:CHUNK] if dirn == 0 else dr[:, 0:1]))
        for x in chains:
            km = jnp.where(x["own"], x["pair"]["kb"], zero_b)
            qm = jnp.where(x["own"], x["pair"]["qb"], zero_b)
            x["gp"] = _dot_nt(jnp.concatenate([km, qm], axis=0), jnp.concatenate([km, km], axis=0))
        pms = []
        for x in chains:
            gam = jnp.exp(x["dc"] - x["dr2"])
            x["e_dc"] = jnp.exp(x["dc"])
            gp = x["gp"]
            pms.append(jnp.where(keep[x["dirn"]] & ~lo, -(gp[0:CHUNK] * x["beta"] * gam), eye_lo))
            x["qk"] = jnp.where(keep[x["dirn"]], gp[CHUNK:2 * CHUNK] * SCALE * gam, 0.0)[:, 0:HEAD_DIM].astype(BF16)
            x["rhs"] = jnp.concatenate([x["pair"]["v2"] * x["beta"], x["pair"]["k2"] * (x["beta"] * x["e_dc"])],
                                       axis=-1).astype(BF16)
        for s in range(6):
            pmbs = [pm.astype(BF16) for pm in pms]
            mrs = [_dot(pmb, jnp.concatenate([zero_b, pmb], axis=0)) for pmb in pmbs]
            pms = [jnp.where(lo, pm + mr, mr) for pm, mr in zip(pms, mrs)]
        uws = [_dot(pm[:, 0:HEAD_DIM].astype(BF16), x["rhs"]) for pm, x in zip(pms, chains)]
        sbs = [st[x["dirn"], x["p"]] for x in pairs]
        sbbs = [s_old.astype(BF16) for s_old in sbs]
        wss = [_dot(uw[:, LANES:2 * LANES].astype(BF16), sbbs[n // 2]) for n, uw in enumerate(uws)]
        qss = [_dot(x["qb"], sbb) for x, sbb in zip(pairs, sbbs)]
        vns = [uw[:, 0:LANES] - ws for uw, ws in zip(uws, wss)]
        o2s = [_dot(x["qk"], vn.astype(BF16)) for x, vn in zip(chains, vns)]
        vss = [vn * jnp.exp(x["dl"] - x["dc"]) for x, vn in zip(chains, vns)]
        for n, x in enumerate(pairs):
            a, b2 = chains[2 * n], chains[2 * n + 1]
            tn = _dot_tn(x["kb"], jnp.where(lo, vss[2 * n], vss[2 * n + 1]).astype(BF16))
            cd = jnp.where(lo_row, jnp.exp(a["dl"]), jnp.exp(b2["dl"]))
            st[x["dirn"], x["p"]] = sbs[n] * cd + jnp.where(same_head, tn, 0.0)
            o = jnp.where(lo, o2s[2 * n] + qss[n] * (SCALE * a["e_dc"]), o2s[2 * n + 1] + qss[n] * (SCALE * b2["e_dc"]))
            ins[x["dirn"]][3][0, pl.ds(x["r0"], CHUNK), x["p"] * LANES:(x["p"] + 1) * LANES] = o
        return carry

    lax.fori_loop(0, nc, chunk_step, 0)


def _delta(proj, gt, acol, dtcol, arow, dtrow):
    b, t, _ = proj.shape
    nb = t // TB
    ba_blk = P_BA // LANES

    def specs(blk_of):
        return [
            pl.BlockSpec((1, TB, QKV_W), lambda bi, i: (bi, blk_of(i), 0)),
            pl.BlockSpec((1, TB, LANES), lambda bi, i: (bi, blk_of(i), ba_blk)),
            pl.BlockSpec((1, BA_ROWS, TB), lambda bi, i: (bi, 0, blk_of(i))),
        ]

    fwd = lambda i: i
    bwd = lambda i: nb - 1 - i
    const = lambda shape: pl.BlockSpec(shape, lambda bi, i: (0, 0))
    return pl.pallas_call(
        functools.partial(_delta_body, nb),
        grid=(b, nb),
        in_specs=specs(fwd) + specs(bwd) + [const((1, LANES)), const((1, LANES)), const((BA_ROWS, 1)), const((BA_ROWS, 1))],
        out_specs=[
            pl.BlockSpec((1, TB, A_WIDTH), lambda bi, i: (bi, i, 0)),
            pl.BlockSpec((1, TB, A_WIDTH), lambda bi, i: (bi, nb - 1 - i, 0)),
        ],
        out_shape=[jax.ShapeDtypeStruct((b, t, A_WIDTH), F32)] * 2,
        scratch_shapes=[
            pltpu.VMEM((2, TB, LANES), F32),
            pltpu.VMEM((2, TB, LANES), F32),
            pltpu.VMEM((2, TB // CHUNK, BA_ROWS, CHUNK), F32),
            pltpu.VMEM((2, A_HEADS // 2, LANES, LANES), F32),
        ],
        compiler_params=_params(("parallel", "arbitrary")),
        name="delta",
    )(proj, proj, gt, proj, proj, gt, acol, dtcol, arow, dtrow)


def _retention_body(nb, qf, kf, vf, qb_, kb_, vb_, z_ref, of_ref, ob_ref, dmat, tab, st):
    i = pl.program_id(1)
    nc = TB // RCHUNK
    npair = B_HEADS // 2

    @pl.when(i == 0)
    def _():
        st[...] = jnp.zeros_like(st)

    lg = -_softplus(-z_ref[...])
    ri = lax.broadcasted_iota(jnp.int32, (RCHUNK, RCHUNK), 0)
    ci = lax.broadcasted_iota(jnp.int32, (RCHUNK, RCHUNK), 1)
    diff = (ri - ci).astype(F32)
    pos = ri.astype(F32)
    lo = ci < HEAD_DIM
    same_head = (ri < HEAD_DIM) == lo
    for h in range(B_HEADS):
        lgf = lg[0:1, h:h + 1]
        lgb = lg[1:2, h:h + 1]
        dmat[h] = SCALE * (jnp.where(ri >= ci, jnp.exp(diff * lgf), 0.0) + jnp.where(ci >= ri, jnp.exp(-diff * lgb), 0.0))
    cds = []
    for p in range(npair):
        lgf = jnp.where(lo, lg[0:1, 2 * p:2 * p + 1], lg[0:1, 2 * p + 1:2 * p + 2])
        lgb = jnp.where(lo, lg[1:2, 2 * p:2 * p + 1], lg[1:2, 2 * p + 1:2 * p + 2])
        tab[p, 0] = SCALE * jnp.exp((pos + 1.0) * lgf)
        tab[p, 1] = jnp.exp((RCHUNK - 1.0 - pos) * lgf)
        tab[p, 2] = SCALE * jnp.exp((RCHUNK - pos) * lgb)
        tab[p, 3] = jnp.exp(pos * lgb)
        cds.append((jnp.exp(RCHUNK * lgf[0:1, :]), jnp.exp(RCHUNK * lgb[0:1, :])))

    def chunk_step(c, carry):
        rf = pl.multiple_of(c * RCHUNK, RCHUNK)
        rb = pl.multiple_of((nc - 1 - c) * RCHUNK, RCHUNK)
        loads = []
        for p in range(npair):
            cols = slice(p * LANES, (p + 1) * LANES)
            loads.append((qf[0, pl.ds(rf, RCHUNK), cols].astype(BF16), kf[0, pl.ds(rf, RCHUNK), cols].astype(BF16),
                          vf[0, pl.ds(rf, RCHUNK), cols], qb_[0, pl.ds(rb, RCHUNK), cols].astype(BF16),
                          kb_[0, pl.ds(rb, RCHUNK), cols].astype(BF16), vb_[0, pl.ds(rb, RCHUNK), cols]))
        zero = jnp.zeros((RCHUNK, LANES), BF16)
        s_lo = [_dot_nt(x[0], jnp.where(lo, x[1], zero)) for x in loads]
        s_hi = [_dot_nt(x[0], jnp.where(lo, zero, x[1])) for x in loads]
        qsf = [_dot(x[0], st[0, p].astype(BF16)) for p, x in enumerate(loads)]
        qsb = [_dot(x[3], st[1, p].astype(BF16)) for p, x in enumerate(loads)]
        tnf = [_dot_tn(x[1], (x[2] * tab[p, 1]).astype(BF16)) for p, x in enumerate(loads)]
        tnb = [_dot_tn(x[4], (x[5] * tab[p, 3]).astype(BF16)) for p, x in enumerate(loads)]
        intra = []
        for p, x in enumerate(loads):
            vb16 = x[2].astype(BF16)
            pm = jnp.concatenate([(s_lo[p] * dmat[2 * p]).astype(BF16), (s_hi[p] * dmat[2 * p + 1]).astype(BF16)], axis=-1)
            v2 = jnp.concatenate([jnp.where(lo, vb16, zero), jnp.where(lo, zero, vb16)], axis=0)
            intra.append(_dot(pm, v2))
        for p in range(npair):
            cols = slice(p * LANES, (p + 1) * LANES)
            of_ref[0, pl.ds(rf, RCHUNK), cols] = intra[p] + qsf[p] * tab[p, 0]
            ob_ref[0, pl.ds(rb, RCHUNK), cols] = qsb[p] * tab[p, 2]
            st[0, p] = st[0, p] * cds[p][0] + jnp.where(same_head, tnf[p], 0.0)
            st[1, p] = st[1, p] * cds[p][1] + jnp.where(same_head, tnb[p], 0.0)
        return carry

    lax.fori_loop(0, nc, chunk_step, 0)


def _retention(proj, z):
    b, t, _ = proj.shape
    nb = t // TB

    def spec(col, blk_of):
        return pl.BlockSpec((1, TB, B_WIDTH), lambda bi, i: (bi, blk_of(i), col // B_WIDTH))

    fwd = lambda i: i
    bwd = lambda i: nb - 1 - i
    return pl.pallas_call(
        functools.partial(_retention_body, nb),
        grid=(b, nb),
        in_specs=[spec(P_BQ, fwd), spec(P_BK, fwd), spec(P_BV, fwd), spec(P_BQ, bwd), spec(P_BK, bwd), spec(P_BV, bwd),
                  pl.BlockSpec((SUBLANES, LANES), lambda bi, i: (0, 0))],
        out_specs=[
            pl.BlockSpec((1, TB, B_WIDTH), lambda bi, i: (bi, i, 0)),
            pl.BlockSpec((1, TB, B_WIDTH), lambda bi, i: (bi, nb - 1 - i, 0)),
        ],
        out_shape=[jax.ShapeDtypeStruct((b, t, B_WIDTH), F32)] * 2,
        scratch_shapes=[
            pltpu.VMEM((B_HEADS, RCHUNK, RCHUNK), F32),
            pltpu.VMEM((B_HEADS // 2, 4, RCHUNK, LANES), F32),
            pltpu.VMEM((2, B_HEADS // 2, LANES, LANES), F32),
        ],
        compiler_params=_params(("parallel", "arbitrary")),
        name="retention",
    )(proj, proj, proj, proj, proj, proj, z)


def _attention_body(nq, q_ref, km, kp, kn, vm, vp, vn, gate_ref, sink_ref, o_ref, kx, vx, bias):
    i = pl.program_id(1)
    nqb = TB // QBLOCK
    lo_k = lax.broadcasted_iota(jnp.int32, (TB + 2 * QBLOCK, LANES), 1) < HEAD_DIM
    for src, dst in (((kp, km, kn), kx), ((vp, vm, vn), vx)):
        full = jnp.concatenate([src[0][0], src[1][0], src[2][0]], axis=0)
        swapped = pltpu.roll(full, HEAD_DIM, 1)
        dst[0] = jnp.where(lo_k, full, swapped).astype(BF16)
        dst[1] = jnp.where(lo_k, swapped, full).astype(BF16)

    qi = lax.broadcasted_iota(jnp.int32, (QBLOCK, 3 * QBLOCK), 0)
    kj = lax.broadcasted_iota(jnp.int32, (QBLOCK, 3 * QBLOCK), 1)
    rel = jnp.abs(kj - QBLOCK - qi)
    relf = rel.astype(F32)
    for hq in range(C_HEADS):
        slope = 2.0 ** (-8.0 * (hq + 1) / C_HEADS)
        bias[hq] = jnp.where(rel <= WINDOW, -slope * relf, -jnp.inf)
    first_lo = jnp.where(i == 0, QBLOCK, 0)
    last_hi = jnp.where(i == nq - 1, 2 * QBLOCK, 3 * QBLOCK)
    sink_all = sink_ref[...]
    lo_q = lax.broadcasted_iota(jnp.int32, (QBLOCK, LANES), 1) < HEAD_DIM
    zero = jnp.zeros((QBLOCK, LANES), BF16)

    items = [(qb, g) for qb in range(nqb) for g in range(C_KV_HEADS)]

    def scores(item):
        qb, g = item
        q = (q_ref[0, qb * QBLOCK:(qb + 1) * QBLOCK, g * LANES:(g + 1) * LANES] * SCALE).astype(BF16)
        qm = jnp.concatenate([jnp.where(lo_q, q, zero), jnp.where(lo_q, zero, q)], axis=0)
        return _dot_nt(qm, kx[g, qb * QBLOCK:(qb + 3) * QBLOCK, :])

    s_next = scores(items[0])
    for n, (qb, g) in enumerate(items):
        s_all = s_next
        if n + 1 < len(items):
            s_next = scores(items[n + 1])
        ps, dens = [], []
        for j in range(C_GROUP):
            hq = g * C_GROUP + j
            s = s_all[j * QBLOCK:(j + 1) * QBLOCK] + bias[hq]
            if qb == 0:
                s = jnp.where(kj >= first_lo, s, -jnp.inf)
            if qb == nqb - 1:
                s = jnp.where(kj < last_hi, s, -jnp.inf)
            sink = sink_all[:, hq:hq + 1]
            m = jnp.maximum(jnp.max(s, axis=-1, keepdims=True), sink)
            p = jnp.exp(s - m)
            dens.append(jnp.sum(p, axis=-1, keepdims=True) + jnp.exp(sink - m))
            ps.append(p.astype(BF16))
        o2 = _dot(jnp.concatenate(ps, axis=0), vx[g, qb * QBLOCK:(qb + 3) * QBLOCK, :])
        o = jnp.where(lo_q, o2[0:QBLOCK] * (1.0 / dens[0]), o2[QBLOCK:2 * QBLOCK] * (1.0 / dens[1]))
        rows = slice(qb * QBLOCK, (qb + 1) * QBLOCK)
        cols = slice(g * LANES, (g + 1) * LANES)
        o_ref[0, rows, cols] = o * _silu(gate_ref[0, rows, cols])


def _attention(proj, sink):
    b, t, _ = proj.shape
    nq = t // TB
    per = TB // QBLOCK
    last = t // QBLOCK - 1

    def halo(col):
        cb = col // C_KV_WIDTH
        return [
            pl.BlockSpec((1, TB, C_KV_WIDTH), lambda bi, i: (bi, i, cb)),
            pl.BlockSpec((1, QBLOCK, C_KV_WIDTH), lambda bi, i: (bi, jnp.maximum(i * per - 1, 0), cb)),
            pl.BlockSpec((1, QBLOCK, C_KV_WIDTH), lambda bi, i: (bi, jnp.minimum((i + 1) * per, last), cb)),
        ]

    return pl.pallas_call(
        functools.partial(_attention_body, nq),
        grid=(b, nq),
        in_specs=[pl.BlockSpec((1, TB, C_WIDTH), lambda bi, i: (bi, i, P_CQ // C_WIDTH))] + halo(P_CK) + halo(P_CV) + [
            pl.BlockSpec((1, TB, C_WIDTH), lambda bi, i: (bi, i, P_CG // C_WIDTH)),
            pl.BlockSpec((1, LANES), lambda bi, i: (0, 0))],
        out_specs=pl.BlockSpec((1, TB, C_WIDTH), lambda bi, i: (bi, i, 0)),
        out_shape=jax.ShapeDtypeStruct((b, t, C_WIDTH), F32),
        scratch_shapes=[
            pltpu.VMEM((C_KV_HEADS, TB + 2 * QBLOCK, LANES), BF16),
            pltpu.VMEM((C_KV_HEADS, TB + 2 * QBLOCK, LANES), BF16),
            pltpu.VMEM((C_HEADS, QBLOCK, 3 * QBLOCK), F32),
        ],
        compiler_params=_params(("parallel", "parallel")),
        name="attention",
    )(proj, proj, proj, proj, proj, proj, proj, proj, sink)


def _out_proj_body(final, h_ref, af, ab, ag, bf, bb, bg, mc, p_ref, dng, rng, wo, wple, wpg, fg, o_ref):
    lo_lane = _pair_masks(TB)
    inv = 1.0 / HEAD_DIM

    def head_stat(x):
        outs = []
        for p in range(x.shape[-1] // LANES):
            xb = x[:, p * LANES:(p + 1) * LANES]
            s_lo = jnp.sum(jnp.where(lo_lane, xb, 0.0), axis=-1, keepdims=True)
            s_hi = jnp.sum(jnp.where(lo_lane, 0.0, xb), axis=-1, keepdims=True)
            outs.append(jnp.where(lo_lane, s_lo, s_hi) * inv)
        return jnp.concatenate(outs, axis=-1)

    oa = af[0] + ab[0]
    ya = oa * lax.rsqrt(head_stat(oa * oa) + EPS) * dng[...] * _silu(ag[0])
    ob = bf[0] + bb[0]
    xc = ob - head_stat(ob)
    yb = xc * lax.rsqrt(head_stat(xc * xc) + EPS) * rng[...] * _silu(bg[0])
    mix = jnp.concatenate([ya, yb, mc[0]], axis=-1).astype(BF16)
    h1 = h_ref[0] + _dot(mix, wo[...])
    h2 = h1 + _dot(p_ref[0, 0].astype(BF16), wple[...]) * _sigmoid(_dot(h1.astype(BF16), wpg[...]))
    if final:
        ms = jnp.mean(h2 * h2, axis=-1, keepdims=True)
        h2 = h2 * lax.rsqrt(ms + EPS) * fg[...]
    o_ref[0] = h2


def _out_proj(final, layer, h, oaf, oab, obf, obb, mixc, proj, p, dng, rng, wo, wple, wpg, fg):
    b, t, _ = h.shape
    row = lambda w: pl.BlockSpec((1, TB, w), lambda bi, i: (bi, i, 0))
    col = lambda w, c: pl.BlockSpec((1, TB, w), lambda bi, i: (bi, i, c // w))
    const = lambda shape: pl.BlockSpec(shape, lambda bi, i: (0, 0))
    return pl.pallas_call(
        functools.partial(_out_proj_body, final),
        grid=(b, t // TB),
        in_specs=[row(D_MODEL), row(A_WIDTH), row(A_WIDTH), col(A_WIDTH, P_AG), row(B_WIDTH), row(B_WIDTH), col(B_WIDTH, P_BG),
                  row(C_WIDTH), pl.BlockSpec((1, 1, TB, PLE_DIM), lambda bi, i: (layer, bi, i, 0)),
                  const((1, A_WIDTH)), const((1, B_WIDTH)), const((D_MODEL, D_MODEL)), const((PLE_DIM, D_MODEL)),
                  const((D_MODEL, D_MODEL)), const((1, D_MODEL))],
        out_specs=row(D_MODEL),
        out_shape=jax.ShapeDtypeStruct((b, t, D_MODEL), F32),
        compiler_params=_params(("parallel", "parallel")),
        name="out_proj",
    )(h, oaf, oab, proj, obf, obb, proj, mixc, p, dng, rng, wo, wple, wpg, fg)


def _layer_weights(i, w_in, w_out, norm_g, conv_w, dn_a_log, dn_dt_bias, dn_norm_g, ret_decay_z, ret_norm_g, attn_sink,
                   w_ple, w_pg):
    w = w_in[i]
    w_p = jnp.concatenate([w[:, :ORIG_BA], w[:, ORIG_BA + N_BA:], w[:, ORIG_BA:ORIG_BA + N_BA],
                           jnp.zeros((D_MODEL, LANES - N_BA), F32)], axis=-1).astype(BF16)
    w_ba_t = jnp.pad(w[:, ORIG_BA:ORIG_BA + N_BA].T, ((0, BA_ROWS - N_BA), (0, 0))).astype(BF16)
    a = jnp.exp(dn_a_log[i].astype(F32)).reshape(-1)
    dt = dn_dt_bias[i].astype(F32).reshape(-1)
    place = lambda x, n: jnp.zeros((n,), F32).at[G_OFF:G_OFF + 2 * A_HEADS].set(x)
    return dict(
        norm_g=norm_g[i].reshape(1, D_MODEL).astype(F32), w_p=w_p, w_ba_t=w_ba_t,
        cw=jnp.pad(conv_w[i].astype(F32), ((0, SUBLANES - CONV_K), (0, 0))),
        acol=place(a, LANES).reshape(1, LANES), dtcol=place(dt, LANES).reshape(1, LANES),
        arow=place(a, BA_ROWS).reshape(BA_ROWS, 1), dtrow=place(dt, BA_ROWS).reshape(BA_ROWS, 1),
        z=jnp.pad(ret_decay_z[i].astype(F32), ((0, SUBLANES - 2), (0, LANES - B_HEADS))),
        sink=jnp.pad(attn_sink[i].astype(F32), (0, LANES - C_HEADS)).reshape(1, LANES),
        dng=jnp.tile(dn_norm_g[i].astype(F32), A_HEADS).reshape(1, A_WIDTH),
        rng=ret_norm_g[i].astype(F32).reshape(1, B_WIDTH),
        wo=w_out[i].astype(BF16), wple=w_ple[i].astype(BF16), wpg=w_pg[i].astype(BF16),
    )


def _trunk(x, p, layers, final_g):
    h = x
    fg = final_g.reshape(1, D_MODEL).astype(F32)
    for i, lw in enumerate(layers):
        proj, gt = _in_proj(h, lw["norm_g"], lw["w_p"], lw["w_ba_t"], lw["cw"])
        oaf, oab = _delta(proj, gt, lw["acol"], lw["dtcol"], lw["arow"], lw["dtrow"])
        obf, obb = _retention(proj, lw["z"])
        mixc = _attention(proj, lw["sink"])
        h = _out_proj(i == len(layers) - 1, i, h, oaf, oab, obf, obb, mixc, proj, p, lw["dng"], lw["rng"],
                      lw["wo"], lw["wple"], lw["wpg"], fg)
    return h


def kernel(x_prompt, x_sample, p_prompt, p_sample, w_in, w_out, norm_g, conv_w, dn_a_log, dn_dt_bias, dn_norm_g,
           ret_decay_z, ret_norm_g, attn_sink, w_ple, w_pg, final_g):
    depth = w_in.shape[0]
    layers = [_layer_weights(i, w_in, w_out, norm_g, conv_w, dn_a_log, dn_dt_bias, dn_norm_g, ret_decay_z, ret_norm_g,
                             attn_sink, w_ple, w_pg) for i in range(depth)]
    y_prompt = _trunk(x_prompt, p_prompt, layers, final_g)
    y_sample = _trunk(x_sample, p_sample, layers, final_g)
    return (y_prompt, y_sample)
```

```python
import functools

import jax
import jax.numpy as jnp
from jax import lax
from jax.experimental import pallas as pl
from jax.experimental.pallas import tpu as pltpu

F32 = jnp.float32
BF16 = jnp.bfloat16

D_MODEL = 1024
HEAD_DIM = 64
A_HEADS = 6
A_WIDTH = A_HEADS * HEAD_DIM
B_HEADS = 6
B_WIDTH = B_HEADS * HEAD_DIM
C_HEADS = 4
C_KV_HEADS = 2
C_GROUP = C_HEADS // C_KV_HEADS
C_WIDTH = C_HEADS * HEAD_DIM
C_KV_WIDTH = C_KV_HEADS * HEAD_DIM
CONV_K = 5
WINDOW = 128
PLE_DIM = 256
EPS = 1e-6
SCALE = HEAD_DIM ** -0.5

LANES = 128
SUBLANES = 8
VMEM_LIMIT = 48 * 1024 * 1024

QKV_W = 3 * A_WIDTH
P_AG = QKV_W
P_BQ = P_AG + A_WIDTH
P_BK = P_BQ + B_WIDTH
P_BV = P_BK + B_WIDTH
P_BG = P_BV + B_WIDTH
P_CQ = P_BG + B_WIDTH
P_CK = P_CQ + C_WIDTH
P_CV = P_CK + C_KV_WIDTH
P_CG = P_CV + C_KV_WIDTH
P_BA = P_CG + C_WIDTH
PROJ_W = P_BA + LANES
ORIG_BA = QKV_W + A_WIDTH
N_BA = 4 * A_HEADS
BA_ROWS = 32
G_OFF = 2 * A_HEADS

CHUNK = 64
RCHUNK = 128
QBLOCK = 128
TB = 512
DELTA_UNROLL = 2
ATTN_GROUP = 4

NT_DIMS = (((1,), (1,)), ((), ()))
TN_DIMS = (((0,), (0,)), ((), ()))


def _dot(a, b):
    return jnp.dot(a, b, preferred_element_type=F32)


def _dot_nt(a, b):
    return lax.dot_general(a, b, NT_DIMS, preferred_element_type=F32)


def _dot_tn(a, b):
    return lax.dot_general(a, b, TN_DIMS, preferred_element_type=F32)


def _sigmoid(x):
    return 0.5 * jnp.tanh(0.5 * x) + 0.5


def _silu(x):
    return x * _sigmoid(x)


def _softplus(x):
    return jnp.maximum(x, 0.0) + jnp.log1p(jnp.exp(-jnp.abs(x)))


def _split3(x):
    hi = x.astype(BF16)
    r1 = x - hi.astype(F32)
    mid = r1.astype(BF16)
    lo = (r1 - mid.astype(F32)).astype(BF16)
    return hi, mid, lo


def _pair_masks(rows):
    lane = lax.broadcasted_iota(jnp.int32, (rows, LANES), 1)
    return lane < HEAD_DIM


def _params(sem, flags=None):
    return pltpu.CompilerParams(dimension_semantics=sem, vmem_limit_bytes=VMEM_LIMIT, flags=flags)


def _in_proj_body(nb, h_ref, hp_ref, hn_ref, g_ref, w_ref, wba_ref, cw_ref, proj_ref, gt_ref, xs):
    i = pl.program_id(1)
    x = jnp.concatenate([hp_ref[0], h_ref[0], hn_ref[0]], axis=0)
    ms = jnp.mean(x * x, axis=-1, keepdims=True)
    xn_ext = (x * lax.rsqrt(ms + EPS) * g_ref[...]).astype(BF16)
    xn = xn_ext[SUBLANES:TB + SUBLANES]
    row = lax.broadcasted_iota(jnp.int32, (TB + 2 * SUBLANES, 1), 0)
    inside = ((row >= SUBLANES) | (i > 0)) & ((row < TB + SUBLANES) | (i < nb - 1))
    lo_lane = _pair_masks(CHUNK)
    cw = cw_ref[...]
    base = SUBLANES - CONV_K // 2
    for part in range(3):
        c0 = part * A_WIDTH
        xs[part] = jnp.where(inside, _dot(xn_ext, w_ref[:, c0:c0 + A_WIDTH]), 0.0)
    for c0 in range(QKV_W, PROJ_W, 512):
        c1 = min(c0 + 512, PROJ_W)
        proj_ref[0, :, c0:c1] = _dot(xn, w_ref[:, c0:c1])
    gt_ref[0] = _dot_nt(wba_ref[...], xn)
    for part in range(3):
        c0 = part * A_WIDTH
        for r0 in range(0, TB, CHUNK):
            win = xs[part, r0:r0 + CHUNK + 2 * SUBLANES, :]
            acc = None
            for k in range(CONV_K):
                shifted = win if k == CONV_K // 2 else pltpu.roll(win, (CONV_K // 2 - k) % (CHUNK + 2 * SUBLANES), 0)
                term = cw[k:k + 1, c0:c0 + A_WIDTH] * shifted[SUBLANES:SUBLANES + CHUNK]
                acc = term if acc is None else acc + term
            y = _silu(acc)
            if part < 2:
                for p in range(A_WIDTH // LANES):
                    y2 = y[:, p * LANES:(p + 1) * LANES]
                    sq = y2 * y2
                    s_lo = jnp.sum(jnp.where(lo_lane, sq, 0.0), axis=-1, keepdims=True)
                    s_hi = jnp.sum(jnp.where(lo_lane, 0.0, sq), axis=-1, keepdims=True)
                    r = jnp.where(lo_lane, lax.rsqrt(s_lo + EPS), lax.rsqrt(s_hi + EPS))
                    proj_ref[0, r0:r0 + CHUNK, c0 + p * LANES:c0 + (p + 1) * LANES] = y2 * r
            else:
                proj_ref[0, r0:r0 + CHUNK, c0:c0 + A_WIDTH] = y


def _in_proj(h, norm_g, w_p, w_ba_t, cw):
    b, t, _ = h.shape
    nb = t // TB
    hb = TB // SUBLANES
    last_h = t // SUBLANES - 1
    const = lambda shape: pl.BlockSpec(shape, lambda bi, i: (0, 0))
    return pl.pallas_call(
        functools.partial(_in_proj_body, nb),
        grid=(b, nb),
        in_specs=[
            pl.BlockSpec((1, TB, D_MODEL), lambda bi, i: (bi, i, 0)),
            pl.BlockSpec((1, SUBLANES, D_MODEL), lambda bi, i: (bi, jnp.maximum(i * hb - 1, 0), 0)),
            pl.BlockSpec((1, SUBLANES, D_MODEL), lambda bi, i: (bi, jnp.minimum((i + 1) * hb, last_h), 0)),
            const((1, D_MODEL)), const((D_MODEL, PROJ_W)), const((BA_ROWS, D_MODEL)), const((SUBLANES, QKV_W)),
        ],
        out_specs=[
            pl.BlockSpec((1, TB, PROJ_W), lambda bi, i: (bi, i, 0)),
            pl.BlockSpec((1, BA_ROWS, TB), lambda bi, i: (bi, 0, i)),
        ],
        out_shape=[jax.ShapeDtypeStruct((b, t, PROJ_W), F32), jax.ShapeDtypeStruct((b, BA_ROWS, t), F32)],
        scratch_shapes=[pltpu.VMEM((3, TB + 2 * SUBLANES, A_WIDTH), F32)],
        compiler_params=_params(("parallel", "parallel")),
        name="in_proj",
    )(h, h, h, norm_g, w_p, w_ba_t, cw)


def _delta_body(nb, *refs):
    (qkv_f, ba_f, gt_f, qkv_b, ba_b, gt_b, acol_ref, dtcol_ref, arow_ref, dtrow_ref,
     of_ref, ob_ref, bcol, dcol, drow, st) = refs
    i = pl.program_id(1)
    nc = TB // CHUNK
    npair = A_HEADS // 2

    @pl.when(i == 0)
    def _():
        st[...] = jnp.zeros_like(st)

    ri = lax.broadcasted_iota(jnp.int32, (CHUNK, CHUNK), 0)
    ci = lax.broadcasted_iota(jnp.int32, (CHUNK, CHUNK), 1)
    incl = (ri >= ci, ri <= ci)

    ins = ((qkv_f, ba_f, gt_f, of_ref), (qkv_b, ba_b, gt_b, ob_ref))
    for dirn, (_, ba, gt, _) in enumerate(ins):
        cum = incl[dirn].astype(BF16)
        cum_t = incl[1 - dirn].astype(BF16)
        for c in range(nc):
            r0 = c * CHUNK
            bac = ba[0, r0:r0 + CHUNK, :]
            bcol[dirn, r0:r0 + CHUNK, :] = _sigmoid(bac)
            g_c = -acol_ref[...] * _softplus(bac + dtcol_ref[...])
            h3, m3, l3 = _split3(g_c)
            d3 = _dot(cum, jnp.concatenate([h3, m3, l3], axis=-1))
            dcol[dirn, r0:r0 + CHUNK, :] = (d3[:, 0:LANES] + d3[:, LANES:2 * LANES]) + d3[:, 2 * LANES:3 * LANES]
            gtc = gt[0, :, r0:r0 + CHUNK]
            g_r = -arow_ref[...] * _softplus(gtc + dtrow_ref[...])
            h3, m3, l3 = _split3(g_r)
            d3 = _dot(jnp.concatenate([h3, m3, l3], axis=0), cum_t)
            drow[dirn, c] = (d3[0:BA_ROWS] + d3[BA_ROWS:2 * BA_ROWS]) + d3[2 * BA_ROWS:3 * BA_ROWS]

    ri2 = lax.broadcasted_iota(jnp.int32, (CHUNK, LANES), 0)
    ln2 = lax.broadcasted_iota(jnp.int32, (CHUNK, LANES), 1)
    lo = ln2 < HEAD_DIM
    eye_hi = jnp.where(ri2 == ln2 - HEAD_DIM, 1.0, 0.0)
    keep_qk = (lo & (ri2 >= ln2), lo & (ri2 <= ln2))
    keep_m = (lo & (ri2 > ln2), lo & (ri2 < ln2))
    rs2 = lax.broadcasted_iota(jnp.int32, (LANES, LANES), 0)
    cs2 = lax.broadcasted_iota(jnp.int32, (LANES, LANES), 1)
    same_head = (rs2 < HEAD_DIM) == (cs2 < HEAD_DIM)
    lo_row = lax.broadcasted_iota(jnp.int32, (1, LANES), 1) < HEAD_DIM
    zero_b = jnp.zeros((CHUNK, LANES), BF16)

    def chunk_group(cg, carry):
        pairs, chains = [], []
        for j in range(DELTA_UNROLL):
            c = cg * DELTA_UNROLL + j
            for dirn, (qkv, _, _, _) in enumerate(ins):
                cc = c if dirn == 0 else nc - 1 - c
                r0 = pl.multiple_of(cc * CHUNK, CHUNK)
                beta_all = bcol[dirn, pl.ds(r0, CHUNK), :]
                dcol_all = dcol[dirn, pl.ds(r0, CHUNK), :]
                drow_all = drow[dirn, cc]
                for p in range(npair):
                    cols = lambda part: slice(part * A_WIDTH + p * LANES, part * A_WIDTH + (p + 1) * LANES)
                    q2 = qkv[0, pl.ds(r0, CHUNK), cols(0)]
                    k2 = qkv[0, pl.ds(r0, CHUNK), cols(1)]
                    v2 = qkv[0, pl.ds(r0, CHUNK), cols(2)]
                    pair = dict(j=j, dirn=dirn, p=p, r0=r0, k2=k2, v2=v2, qb=q2.astype(BF16), kb=k2.astype(BF16))
                    pairs.append(pair)
                    for e in range(2):
                        lb = A_HEADS * dirn + 2 * p + e
                        lg = G_OFF + lb
                        dr = drow_all[lg:lg + 1, :]
                        chains.append(dict(
                            pair=pair, e=e, dirn=dirn, own=lo if e == 0 else ~lo,
                            beta=jnp.broadcast_to(beta_all[:, lb:lb + 1], (CHUNK, LANES)),
                            dc=jnp.broadcast_to(dcol_all[:, lg:lg + 1], (CHUNK, LANES)),
                            dr2=jnp.concatenate([dr, dr], axis=-1),
                            dl=dr[:, CHUNK - 1:CHUNK] if dirn == 0 else dr[:, 0:1]))
        for x in chains:
            km = jnp.where(x["own"], x["pair"]["kb"], zero_b)
            qm = jnp.where(x["own"], x["pair"]["qb"], zero_b)
            x["gp"] = _dot_nt(jnp.concatenate([km, qm], axis=0), jnp.concatenate([km, km], axis=0))
        pms = []
        for x in chains:
            gam = jnp.exp(x["dc"] - x["dr2"])
            x["e_dc"] = jnp.exp(x["dc"])
            gp = x["gp"]
            pms.append(jnp.where(keep_m[x["dirn"]], -(gp[0:CHUNK] * x["beta"] * gam), eye_hi))
            x["qk"] = jnp.where(keep_qk[x["dirn"]], gp[CHUNK:2 * CHUNK] * SCALE * gam, 0.0)[:, 0:HEAD_DIM].astype(BF16)
            rhs = jnp.concatenate([x["pair"]["v2"] * x["beta"], x["pair"]["k2"] * (x["beta"] * x["e_dc"])], axis=-1)
            x["rhs"] = jnp.concatenate([jnp.zeros((CHUNK, 2 * LANES), BF16), rhs.astype(BF16)], axis=0)
        for s in range(6):
            pmbs = [pm.astype(BF16) for pm in pms]
            mrs = [_dot(pmb[:, 0:HEAD_DIM], pmb) for pmb in pmbs]
            pms = [jnp.where(lo, mr, pm + mr) for pm, mr in zip(pms, mrs)]
        uws = [_dot(pm.astype(BF16), x["rhs"]) for pm, x in zip(pms, chains)]
        per_chunk = 2 * npair
        for j in range(DELTA_UNROLL):
            prs = pairs[j * per_chunk:(j + 1) * per_chunk]
            chs = chains[2 * j * per_chunk:2 * (j + 1) * per_chunk]
            uwj = uws[2 * j * per_chunk:2 * (j + 1) * per_chunk]
            sbs = [st[x["dirn"], x["p"]] for x in prs]
            sbbs = [s_old.astype(BF16) for s_old in sbs]
            wss = [_dot(uw[:, LANES:2 * LANES].astype(BF16), sbbs[n // 2]) for n, uw in enumerate(uwj)]
            qss = [_dot(x["qb"], sbb) for x, sbb in zip(prs, sbbs)]
            vns = [uw[:, 0:LANES] - ws for uw, ws in zip(uwj, wss)]
            o2s = [_dot(x["qk"], vn.astype(BF16)) for x, vn in zip(chs, vns)]
            vss = [vn * jnp.exp(x["dl"] - x["dc"]) for x, vn in zip(chs, vns)]
            for n, x in enumerate(prs):
                a, b2 = chs[2 * n], chs[2 * n + 1]
                tn = _dot_tn(x["kb"], jnp.where(lo, vss[2 * n], vss[2 * n + 1]).astype(BF16))
                cd = jnp.where(lo_row, jnp.exp(a["dl"]), jnp.exp(b2["dl"]))
                st[x["dirn"], x["p"]] = sbs[n] * cd + jnp.where(same_head, tn, 0.0)
                o = jnp.where(lo, o2s[2 * n] + qss[n] * (SCALE * a["e_dc"]),
                              o2s[2 * n + 1] + qss[n] * (SCALE * b2["e_dc"]))
                ins[x["dirn"]][3][0, pl.ds(x["r0"], CHUNK), x["p"] * LANES:(x["p"] + 1) * LANES] = o
        return carry

    lax.fori_loop(0, nc // DELTA_UNROLL, chunk_group, 0)


def _delta(proj, gt, acol, dtcol, arow, dtrow):
    b, t, _ = proj.shape
    nb = t // TB
    ba_blk = P_BA // LANES

    def specs(blk_of):
        return [
            pl.BlockSpec((1, TB, QKV_W), lambda bi, i: (bi, blk_of(i), 0)),
            pl.BlockSpec((1, TB, LANES), lambda bi, i: (bi, blk_of(i), ba_blk)),
            pl.BlockSpec((1, BA_ROWS, TB), lambda bi, i: (bi, 0, blk_of(i))),
        ]

    fwd = lambda i: i
    bwd = lambda i: nb - 1 - i
    const = lambda shape: pl.BlockSpec(shape, lambda bi, i: (0, 0))
    return pl.pallas_call(
        functools.partial(_delta_body, nb),
        grid=(b, nb),
        in_specs=specs(fwd) + specs(bwd) + [const((1, LANES)), const((1, LANES)), const((BA_ROWS, 1)), const((BA_ROWS, 1))],
        out_specs=[
            pl.BlockSpec((1, TB, A_WIDTH), lambda bi, i: (bi, i, 0)),
            pl.BlockSpec((1, TB, A_WIDTH), lambda bi, i: (bi, nb - 1 - i, 0)),
        ],
        out_shape=[jax.ShapeDtypeStruct((b, t, A_WIDTH), F32)] * 2,
        scratch_shapes=[
            pltpu.VMEM((2, TB, LANES), F32),
            pltpu.VMEM((2, TB, LANES), F32),
            pltpu.VMEM((2, TB // CHUNK, BA_ROWS, CHUNK), F32),
            pltpu.VMEM((2, A_HEADS // 2, LANES, LANES), F32),
        ],
        compiler_params=_params(("parallel", "arbitrary")),
        name="delta",
    )(proj, proj, gt, proj, proj, gt, acol, dtcol, arow, dtrow)


def _retention_tables_body(z_ref, dmat, tab):
    lg = -_softplus(-z_ref[...])
    ri = lax.broadcasted_iota(jnp.int32, (RCHUNK, RCHUNK), 0)
    ci = lax.broadcasted_iota(jnp.int32, (RCHUNK, RCHUNK), 1)
    diff = (ri - ci).astype(F32)
    pos = ri.astype(F32)
    lo = ci < HEAD_DIM
    for h in range(B_HEADS):
        lgf = lg[0:1, h:h + 1]
        lgb = lg[1:2, h:h + 1]
        dmat[h] = SCALE * (jnp.where(ri >= ci, jnp.exp(diff * lgf), 0.0) + jnp.where(ci >= ri, jnp.exp(-diff * lgb), 0.0))
    for p in range(B_HEADS // 2):
        lgf = jnp.where(lo, lg[0:1, 2 * p:2 * p + 1], lg[0:1, 2 * p + 1:2 * p + 2])
        lgb = jnp.where(lo, lg[1:2, 2 * p:2 * p + 1], lg[1:2, 2 * p + 1:2 * p + 2])
        tab[p, 0] = SCALE * jnp.exp((pos + 1.0) * lgf)
        tab[p, 1] = jnp.exp((RCHUNK - 1.0 - pos) * lgf)
        tab[p, 2] = SCALE * jnp.exp((RCHUNK - pos) * lgb)
        tab[p, 3] = jnp.exp(pos * lgb)
        tab[p, 4] = jnp.exp(RCHUNK * lgf)
        tab[p, 5] = jnp.exp(RCHUNK * lgb)


def _retention_tables(z):
    return pl.pallas_call(
        _retention_tables_body,
        out_shape=[jax.ShapeDtypeStruct((B_HEADS, RCHUNK, RCHUNK), F32),
                   jax.ShapeDtypeStruct((B_HEADS // 2, 6, RCHUNK, LANES), F32)],
        name="retention_tables",
    )(z)


def _retention_body(nb, qf, kf, vf, qb_, kb_, vb_, dmat, tab, of_ref, ob_ref, st):
    i = pl.program_id(1)
    nc = TB // RCHUNK
    npair = B_HEADS // 2

    @pl.when(i == 0)
    def _():
        st[...] = jnp.zeros_like(st)

    ri = lax.broadcasted_iota(jnp.int32, (RCHUNK, RCHUNK), 0)
    ci = lax.broadcasted_iota(jnp.int32, (RCHUNK, RCHUNK), 1)
    lo = ci < HEAD_DIM
    same_head = (ri < HEAD_DIM) == lo

    def chunk_step(c, carry):
        rf = pl.multiple_of(c * RCHUNK, RCHUNK)
        rb = pl.multiple_of((nc - 1 - c) * RCHUNK, RCHUNK)
        loads = []
        for p in range(npair):
            cols = slice(p * LANES, (p + 1) * LANES)
            loads.append((qf[0, pl.ds(rf, RCHUNK), cols].astype(BF16), kf[0, pl.ds(rf, RCHUNK), cols].astype(BF16),
                          vf[0, pl.ds(rf, RCHUNK), cols], qb_[0, pl.ds(rb, RCHUNK), cols].astype(BF16),
                          kb_[0, pl.ds(rb, RCHUNK), cols].astype(BF16), vb_[0, pl.ds(rb, RCHUNK), cols]))
        zero = jnp.zeros((RCHUNK, LANES), BF16)
        s_lo = [_dot_nt(x[0], jnp.where(lo, x[1], zero)) for x in loads]
        s_hi = [_dot_nt(x[0], jnp.where(lo, zero, x[1])) for x in loads]
        qsf = [_dot(x[0], st[0, p].astype(BF16)) for p, x in enumerate(loads)]
        qsb = [_dot(x[3], st[1, p].astype(BF16)) for p, x in enumerate(loads)]
        tnf = [_dot_tn(x[1], (x[2] * tab[p, 1]).astype(BF16)) for p, x in enumerate(loads)]
        tnb = [_dot_tn(x[4], (x[5] * tab[p, 3]).astype(BF16)) for p, x in enumerate(loads)]
        intra = []
        for p, x in enumerate(loads):
            vb16 = x[2].astype(BF16)
            pm = jnp.concatenate([(s_lo[p] * dmat[2 * p]).astype(BF16), (s_hi[p] * dmat[2 * p + 1]).astype(BF16)], axis=-1)
            v2 = jnp.concatenate([jnp.where(lo, vb16, zero), jnp.where(lo, zero, vb16)], axis=0)
            intra.append(_dot(pm, v2))
        for p in range(npair):
            cols = slice(p * LANES, (p + 1) * LANES)
            of_ref[0, pl.ds(rf, RCHUNK), cols] = intra[p] + qsf[p] * tab[p, 0]
            ob_ref[0, pl.ds(rb, RCHUNK), cols] = qsb[p] * tab[p, 2]
            st[0, p] = st[0, p] * tab[p, 4] + jnp.where(same_head, tnf[p], 0.0)
            st[1, p] = st[1, p] * tab[p, 5] + jnp.where(same_head, tnb[p], 0.0)
        return carry

    lax.fori_loop(0, nc, chunk_step, 0)


def _retention(proj, dmat, tab):
    b, t, _ = proj.shape
    nb = t // TB

    def spec(col, blk_of):
        return pl.BlockSpec((1, TB, B_WIDTH), lambda bi, i: (bi, blk_of(i), col // B_WIDTH))

    fwd = lambda i: i
    bwd = lambda i: nb - 1 - i
    return pl.pallas_call(
        functools.partial(_retention_body, nb),
        grid=(b, nb),
        in_specs=[spec(P_BQ, fwd), spec(P_BK, fwd), spec(P_BV, fwd), spec(P_BQ, bwd), spec(P_BK, bwd), spec(P_BV, bwd),
                  pl.BlockSpec(dmat.shape, lambda bi, i: (0, 0, 0)), pl.BlockSpec(tab.shape, lambda bi, i: (0, 0, 0, 0))],
        out_specs=[
            pl.BlockSpec((1, TB, B_WIDTH), lambda bi, i: (bi, i, 0)),
            pl.BlockSpec((1, TB, B_WIDTH), lambda bi, i: (bi, nb - 1 - i, 0)),
        ],
        out_shape=[jax.ShapeDtypeStruct((b, t, B_WIDTH), F32)] * 2,
        scratch_shapes=[
            pltpu.VMEM((2, B_HEADS // 2, LANES, LANES), F32),
        ],
        compiler_params=_params(("parallel", "arbitrary")),
        name="retention",
    )(proj, proj, proj, proj, proj, proj, dmat, tab)


def _attention_body(nq, q_ref, km, kp, kn, vm, vp, vn, gate_ref, sink_ref, o_ref, kx, vx, bias):
    i = pl.program_id(1)
    nqb = TB // QBLOCK
    lo_k = lax.broadcasted_iota(jnp.int32, (TB + 2 * QBLOCK, LANES), 1) < HEAD_DIM
    for src, dst in (((kp, km, kn), kx), ((vp, vm, vn), vx)):
        full = jnp.concatenate([src[0][0], src[1][0], src[2][0]], axis=0)
        swapped = pltpu.roll(full, HEAD_DIM, 1)
        dst[0] = jnp.where(lo_k, full, swapped).astype(BF16)
        dst[1] = jnp.where(lo_k, swapped, full).astype(BF16)

    qi = lax.broadcasted_iota(jnp.int32, (QBLOCK, 3 * QBLOCK), 0)
    kj = lax.broadcasted_iota(jnp.int32, (QBLOCK, 3 * QBLOCK), 1)
    rel = jnp.abs(kj - QBLOCK - qi)
    relf = rel.astype(F32)
    for hq in range(C_HEADS):
        slope = 2.0 ** (-8.0 * (hq + 1) / C_HEADS)
        bias[hq] = jnp.where(rel <= WINDOW, -slope * relf, -jnp.inf)
    first_lo = jnp.where(i == 0, QBLOCK, 0)
    last_hi = jnp.where(i == nq - 1, 2 * QBLOCK, 3 * QBLOCK)
    sink_all = sink_ref[...]
    lo_q = lax.broadcasted_iota(jnp.int32, (QBLOCK, LANES), 1) < HEAD_DIM
    zero = jnp.zeros((QBLOCK, LANES), BF16)

    items = [(qb, g) for qb in range(nqb) for g in range(C_KV_HEADS)]

    def scores(item):
        qb, g = item
        q = (q_ref[0, qb * QBLOCK:(qb + 1) * QBLOCK, g * LANES:(g + 1) * LANES] * SCALE).astype(BF16)
        qm = jnp.concatenate([jnp.where(lo_q, q, zero), jnp.where(lo_q, zero, q)], axis=0)
        return _dot_nt(qm, kx[g, qb * QBLOCK:(qb + 3) * QBLOCK, :])

    for i0 in range(0, len(items), ATTN_GROUP):
        group = items[i0:i0 + ATTN_GROUP]
        s_all = [scores(item) for item in group]
        heads = [(n, j) for n in range(len(group)) for j in range(C_GROUP)]
        ss = []
        for n, j in heads:
            qb, g = group[n]
            s = s_all[n][j * QBLOCK:(j + 1) * QBLOCK] + bias[g * C_GROUP + j]
            if qb == 0:
                s = jnp.where(kj >= first_lo, s, -jnp.inf)
            if qb == nqb - 1:
                s = jnp.where(kj < last_hi, s, -jnp.inf)
            ss.append(s)
        sinks = [sink_all[:, group[n][1] * C_GROUP + j:group[n][1] * C_GROUP + j + 1] for n, j in heads]
        ms = [jnp.maximum(jnp.max(s, axis=-1, keepdims=True), sink) for s, sink in zip(ss, sinks)]
        ps = [jnp.exp(s - m) for s, m in zip(ss, ms)]
        dens = [jnp.sum(p, axis=-1, keepdims=True) + jnp.exp(sink - m) for p, sink, m in zip(ps, sinks, ms)]
        pbs = [p.astype(BF16) for p in ps]
        o2s = [_dot(jnp.concatenate(pbs[C_GROUP * n:C_GROUP * (n + 1)], axis=0), vx[g, qb * QBLOCK:(qb + 3) * QBLOCK, :])
               for n, (qb, g) in enumerate(group)]
        for n, (qb, g) in enumerate(group):
            o = jnp.where(lo_q, o2s[n][0:QBLOCK] * (1.0 / dens[2 * n]), o2s[n][QBLOCK:2 * QBLOCK] * (1.0 / dens[2 * n + 1]))
            rows = slice(qb * QBLOCK, (qb + 1) * QBLOCK)
            cols = slice(g * LANES, (g + 1) * LANES)
            o_ref[0, rows, cols] = o * _silu(gate_ref[0, rows, cols])


def _attention(proj, sink):
    b, t, _ = proj.shape
    nq = t // TB
    per = TB // QBLOCK
    last = t // QBLOCK - 1

    def halo(col):
        cb = col // C_KV_WIDTH
        return [
            pl.BlockSpec((1, TB, C_KV_WIDTH), lambda bi, i: (bi, i, cb)),
            pl.BlockSpec((1, QBLOCK, C_KV_WIDTH), lambda bi, i: (bi, jnp.maximum(i * per - 1, 0), cb)),
            pl.BlockSpec((1, QBLOCK, C_KV_WIDTH), lambda bi, i: (bi, jnp.minimum((i + 1) * per, last), cb)),
        ]

    return pl.pallas_call(
        functools.partial(_attention_body, nq),
        grid=(b, nq),
        in_specs=[pl.BlockSpec((1, TB, C_WIDTH), lambda bi, i: (bi, i, P_CQ // C_WIDTH))] + halo(P_CK) + halo(P_CV) + [
            pl.BlockSpec((1, TB, C_WIDTH), lambda bi, i: (bi, i, P_CG // C_WIDTH)),
            pl.BlockSpec((1, LANES), lambda bi, i: (0, 0))],
        out_specs=pl.BlockSpec((1, TB, C_WIDTH), lambda bi, i: (bi, i, 0)),
        out_shape=jax.ShapeDtypeStruct((b, t, C_WIDTH), F32),
        scratch_shapes=[
            pltpu.VMEM((C_KV_HEADS, TB + 2 * QBLOCK, LANES), BF16),
            pltpu.VMEM((C_KV_HEADS, TB + 2 * QBLOCK, LANES), BF16),
            pltpu.VMEM((C_HEADS, QBLOCK, 3 * QBLOCK), F32),
        ],
        compiler_params=_params(("parallel", "parallel")),
        name="attention",
    )(proj, proj, proj, proj, proj, proj, proj, proj, sink)


def _out_proj_body(final, h_ref, af, ab, ag, bf, bb, bg, mc, p_ref, dng, rng, wo, wple, wpg, fg, o_ref):
    lo_lane = _pair_masks(TB)
    inv = 1.0 / HEAD_DIM

    def head_stat(x):
        outs = []
        for p in range(x.shape[-1] // LANES):
            xb = x[:, p * LANES:(p + 1) * LANES]
            s_lo = jnp.sum(jnp.where(lo_lane, xb, 0.0), axis=-1, keepdims=True)
            s_hi = jnp.sum(jnp.where(lo_lane, 0.0, xb), axis=-1, keepdims=True)
            outs.append(jnp.where(lo_lane, s_lo, s_hi) * inv)
        return jnp.concatenate(outs, axis=-1)

    oa = af[0] + ab[0]
    ya = oa * lax.rsqrt(head_stat(oa * oa) + EPS) * dng[...] * _silu(ag[0])
    ob = bf[0] + bb[0]
    xc = ob - head_stat(ob)
    yb = xc * lax.rsqrt(head_stat(xc * xc) + EPS) * rng[...] * _silu(bg[0])
    mix = jnp.concatenate([ya, yb, mc[0]], axis=-1).astype(BF16)
    h1 = h_ref[0] + _dot(mix, wo[...])
    h2 = h1 + _dot(p_ref[0, 0].astype(BF16), wple[...]) * _sigmoid(_dot(h1.astype(BF16), wpg[...]))
    if final:
        ms = jnp.mean(h2 * h2, axis=-1, keepdims=True)
        h2 = h2 * lax.rsqrt(ms + EPS) * fg[...]
    o_ref[0] = h2


def _out_proj(final, layer, h, oaf, oab, obf, obb, mixc, proj, p, dng, rng, wo, wple, wpg, fg):
    b, t, _ = h.shape
    row = lambda w: pl.BlockSpec((1, TB, w), lambda bi, i: (bi, i, 0))
    col = lambda w, c: pl.BlockSpec((1, TB, w), lambda bi, i: (bi, i, c // w))
    const = lambda shape: pl.BlockSpec(shape, lambda bi, i: (0, 0))
    return pl.pallas_call(
        functools.partial(_out_proj_body, final),
        grid=(b, t // TB),
        in_specs=[row(D_MODEL), row(A_WIDTH), row(A_WIDTH), col(A_WIDTH, P_AG), row(B_WIDTH), row(B_WIDTH), col(B_WIDTH, P_BG),
                  row(C_WIDTH), pl.BlockSpec((1, 1, TB, PLE_DIM), lambda bi, i: (layer, bi, i, 0)),
                  const((1, A_WIDTH)), const((1, B_WIDTH)), const((D_MODEL, D_MODEL)), const((PLE_DIM, D_MODEL)),
                  const((D_MODEL, D_MODEL)), const((1, D_MODEL))],
        out_specs=row(D_MODEL),
        out_shape=jax.ShapeDtypeStruct((b, t, D_MODEL), F32),
        compiler_params=_params(("parallel", "parallel")),
        name="out_proj",
    )(h, oaf, oab, proj, obf, obb, proj, mixc, p, dng, rng, wo, wple, wpg, fg)


def _layer_weights(i, w_in, w_out, norm_g, conv_w, dn_a_log, dn_dt_bias, dn_norm_g, ret_decay_z, ret_norm_g, attn_sink,
                   w_ple, w_pg):
    w = w_in[i]
    w_p = jnp.concatenate([w[:, :ORIG_BA], w[:, ORIG_BA + N_BA:], w[:, ORIG_BA:ORIG_BA + N_BA],
                           jnp.zeros((D_MODEL, LANES - N_BA), F32)], axis=-1).astype(BF16)
    w_ba_t = jnp.pad(w[:, ORIG_BA:ORIG_BA + N_BA].T, ((0, BA_ROWS - N_BA), (0, 0))).astype(BF16)
    a = jnp.exp(dn_a_log[i].astype(F32)).reshape(-1)
    dt = dn_dt_bias[i].astype(F32).reshape(-1)
    place = lambda x, n: jnp.zeros((n,), F32).at[G_OFF:G_OFF + 2 * A_HEADS].set(x)
    dmat, tab = _retention_tables(jnp.pad(ret_decay_z[i].astype(F32), ((0, SUBLANES - 2), (0, LANES - B_HEADS))))
    return dict(
        dmat=dmat, tab=tab,
        norm_g=norm_g[i].reshape(1, D_MODEL).astype(F32), w_p=w_p, w_ba_t=w_ba_t,
        cw=jnp.pad(conv_w[i].astype(F32), ((0, SUBLANES - CONV_K), (0, 0))),
        acol=place(a, LANES).reshape(1, LANES), dtcol=place(dt, LANES).reshape(1, LANES),
        arow=place(a, BA_ROWS).reshape(BA_ROWS, 1), dtrow=place(dt, BA_ROWS).reshape(BA_ROWS, 1),
        sink=jnp.pad(attn_sink[i].astype(F32), (0, LANES - C_HEADS)).reshape(1, LANES),
        dng=jnp.tile(dn_norm_g[i].astype(F32), A_HEADS).reshape(1, A_WIDTH),
        rng=ret_norm_g[i].astype(F32).reshape(1, B_WIDTH),
        wo=w_out[i].astype(BF16), wple=w_ple[i].astype(BF16), wpg=w_pg[i].astype(BF16),
    )


def _trunk(x, p, layers, final_g):
    h = x
    fg = final_g.reshape(1, D_MODEL).astype(F32)
    for i, lw in enumerate(layers):
        proj, gt = _in_proj(h, lw["norm_g"], lw["w_p"], lw["w_ba_t"], lw["cw"])
        oaf, oab = _delta(proj, gt, lw["acol"], lw["dtcol"], lw["arow"], lw["dtrow"])
        obf, obb = _retention(proj, lw["dmat"], lw["tab"])
        mixc = _attention(proj, lw["sink"])
        h = _out_proj(i == len(layers) - 1, i, h, oaf, oab, obf, obb, mixc, proj, p, lw["dng"], lw["rng"],
                      lw["wo"], lw["wple"], lw["wpg"], fg)
    return h


def kernel(x_prompt, x_sample, p_prompt, p_sample, w_in, w_out, norm_g, conv_w, dn_a_log, dn_dt_bias, dn_norm_g,
           ret_decay_z, ret_norm_g, attn_sink, w_ple, w_pg, final_g):
    depth = w_in.shape[0]
    layers = [_layer_weights(i, w_in, w_out, norm_g, conv_w, dn_a_log, dn_dt_bias, dn_norm_g, ret_decay_z, ret_norm_g,
                             attn_sink, w_ple, w_pg) for i in range(depth)]
    y_prompt = _trunk(x_prompt, p_prompt, layers, final_g)
    y_sample = _trunk(x_sample, p_sample, layers, final_g)
    return (y_prompt, y_sample)
```

```python
import functools

import jax
import jax.numpy as jnp
from jax import lax
from jax.experimental import pallas as pl
from jax.experimental.pallas import tpu as pltpu

F32 = jnp.float32
BF16 = jnp.bfloat16

D_MODEL = 1024
HEAD_DIM = 64
A_HEADS = 6
A_WIDTH = A_HEADS * HEAD_DIM
B_HEADS = 6
B_WIDTH = B_HEADS * HEAD_DIM
C_HEADS = 4
C_KV_HEADS = 2
C_GROUP = C_HEADS // C_KV_HEADS
C_WIDTH = C_HEADS * HEAD_DIM
C_KV_WIDTH = C_KV_HEADS * HEAD_DIM
CONV_K = 5
WINDOW = 128
PLE_DIM = 256
EPS = 1e-6
SCALE = HEAD_DIM ** -0.5

LANES = 128
SUBLANES = 8
VMEM_LIMIT = 48 * 1024 * 1024

QKV_W = 3 * A_WIDTH
P_AG = QKV_W
P_BQ = P_AG + A_WIDTH
P_BK = P_BQ + B_WIDTH
P_BV = P_BK + B_WIDTH
P_BG = P_BV + B_WIDTH
P_CQ = P_BG + B_WIDTH
P_CK = P_CQ + C_WIDTH
P_CV = P_CK + C_KV_WIDTH
P_CG = P_CV + C_KV_WIDTH
P_BA = P_CG + C_WIDTH
PROJ_W = P_BA + LANES
ORIG_BA = QKV_W + A_WIDTH
N_BA = 4 * A_HEADS
BA_ROWS = 32
G_OFF = 2 * A_HEADS

CHUNK = 64
RCHUNK = 128
QBLOCK = 128
TB = 512
MIX_TB = 1024
DELTA_UNROLL = 2
ATTN_GROUP = 4

NT_DIMS = (((1,), (1,)), ((), ()))
TN_DIMS = (((0,), (0,)), ((), ()))


def _dot(a, b):
    return jnp.dot(a, b, preferred_element_type=F32)


def _dot_nt(a, b):
    return lax.dot_general(a, b, NT_DIMS, preferred_element_type=F32)


def _dot_tn(a, b):
    return lax.dot_general(a, b, TN_DIMS, preferred_element_type=F32)


def _sigmoid(x):
    return 0.5 * jnp.tanh(0.5 * x) + 0.5


def _silu(x):
    return x * _sigmoid(x)


def _softplus(x):
    return jnp.maximum(x, 0.0) + jnp.log1p(jnp.exp(-jnp.abs(x)))


def _split3(x):
    hi = x.astype(BF16)
    r1 = x - hi.astype(F32)
    mid = r1.astype(BF16)
    lo = (r1 - mid.astype(F32)).astype(BF16)
    return hi, mid, lo


def _pair_masks(rows):
    lane = lax.broadcasted_iota(jnp.int32, (rows, LANES), 1)
    return lane < HEAD_DIM


def _params(sem, flags=None):
    return pltpu.CompilerParams(dimension_semantics=sem, vmem_limit_bytes=VMEM_LIMIT, flags=flags)


def _in_proj_body(nb, h_ref, hp_ref, hn_ref, g_ref, w_ref, wba_ref, cw_ref, proj_ref, gt_ref, xs):
    i = pl.program_id(1)
    x = jnp.concatenate([hp_ref[0], h_ref[0], hn_ref[0]], axis=0)
    ms = jnp.mean(x * x, axis=-1, keepdims=True)
    xn_ext = (x * lax.rsqrt(ms + EPS) * g_ref[...]).astype(BF16)
    xn = xn_ext[SUBLANES:TB + SUBLANES]
    row = lax.broadcasted_iota(jnp.int32, (TB + 2 * SUBLANES, 1), 0)
    inside = ((row >= SUBLANES) | (i > 0)) & ((row < TB + SUBLANES) | (i < nb - 1))
    lo_lane = _pair_masks(CHUNK)
    cw = cw_ref[...]
    base = SUBLANES - CONV_K // 2
    for c0 in range(0, QKV_W, 512):
        c1 = min(c0 + 512, QKV_W)
        xs[:, c0:c1] = jnp.where(inside, _dot(xn_ext, w_ref[:, c0:c1]), 0.0)
    for c0 in range(QKV_W, PROJ_W, 512):
        c1 = min(c0 + 512, PROJ_W)
        proj_ref[0, :, c0:c1] = _dot(xn, w_ref[:, c0:c1])
    gt_ref[0] = _dot_nt(wba_ref[...], xn)
    for part in range(3):
        c0 = part * A_WIDTH
        for r0 in range(0, TB, CHUNK):
            win = xs[r0:r0 + CHUNK + 2 * SUBLANES, c0:c0 + A_WIDTH]
            acc = None
            for k in range(CONV_K):
                shifted = win if k == CONV_K // 2 else pltpu.roll(win, (CONV_K // 2 - k) % (CHUNK + 2 * SUBLANES), 0)
                term = cw[k:k + 1, c0:c0 + A_WIDTH] * shifted[SUBLANES:SUBLANES + CHUNK]
                acc = term if acc is None else acc + term
            y = _silu(acc)
            if part < 2:
                for p in range(A_WIDTH // LANES):
                    y2 = y[:, p * LANES:(p + 1) * LANES]
                    sq = y2 * y2
                    s_lo = jnp.sum(jnp.where(lo_lane, sq, 0.0), axis=-1, keepdims=True)
                    s_hi = jnp.sum(jnp.where(lo_lane, 0.0, sq), axis=-1, keepdims=True)
                    r = jnp.where(lo_lane, lax.rsqrt(s_lo + EPS), lax.rsqrt(s_hi + EPS))
                    proj_ref[0, r0:r0 + CHUNK, c0 + p * LANES:c0 + (p + 1) * LANES] = y2 * r
            else:
                proj_ref[0, r0:r0 + CHUNK, c0:c0 + A_WIDTH] = y


def _in_proj(h, norm_g, w_p, w_ba_t, cw):
    b, t, _ = h.shape
    nb = t // TB
    hb = TB // SUBLANES
    last_h = t // SUBLANES - 1
    const = lambda shape: pl.BlockSpec(shape, lambda bi, i: (0, 0))
    return pl.pallas_call(
        functools.partial(_in_proj_body, nb),
        grid=(b, nb),
        in_specs=[
            pl.BlockSpec((1, TB, D_MODEL), lambda bi, i: (bi, i, 0)),
            pl.BlockSpec((1, SUBLANES, D_MODEL), lambda bi, i: (bi, jnp.maximum(i * hb - 1, 0), 0)),
            pl.BlockSpec((1, SUBLANES, D_MODEL), lambda bi, i: (bi, jnp.minimum((i + 1) * hb, last_h), 0)),
            const((1, D_MODEL)), const((D_MODEL, PROJ_W)), const((BA_ROWS, D_MODEL)), const((SUBLANES, QKV_W)),
        ],
        out_specs=[
            pl.BlockSpec((1, TB, PROJ_W), lambda bi, i: (bi, i, 0)),
            pl.BlockSpec((1, BA_ROWS, TB), lambda bi, i: (bi, 0, i)),
        ],
        out_shape=[jax.ShapeDtypeStruct((b, t, PROJ_W), F32), jax.ShapeDtypeStruct((b, BA_ROWS, t), F32)],
        scratch_shapes=[pltpu.VMEM((TB + 2 * SUBLANES, QKV_W), F32)],
        compiler_params=_params(("parallel", "parallel")),
        name="in_proj",
    )(h, h, h, norm_g, w_p, w_ba_t, cw)


def _delta_body(nb, *refs):
    TB = MIX_TB
    (qkv_f, ba_f, gt_f, qkv_b, ba_b, gt_b, acol_ref, dtcol_ref, arow_ref, dtrow_ref,
     of_ref, ob_ref, bcol, dcol, drow, st) = refs
    i = pl.program_id(1)
    nc = TB // CHUNK
    npair = A_HEADS // 2

    @pl.when(i == 0)
    def _():
        st[...] = jnp.zeros_like(st)

    ri = lax.broadcasted_iota(jnp.int32, (CHUNK, CHUNK), 0)
    ci = lax.broadcasted_iota(jnp.int32, (CHUNK, CHUNK), 1)
    incl = (ri >= ci, ri <= ci)

    ins = ((qkv_f, ba_f, gt_f, of_ref), (qkv_b, ba_b, gt_b, ob_ref))
    for dirn, (_, ba, gt, _) in enumerate(ins):
        cum = incl[dirn].astype(BF16)
        cum_t = incl[1 - dirn].astype(BF16)
        for c in range(nc):
            r0 = c * CHUNK
            bac = ba[0, r0:r0 + CHUNK, :]
            bcol[dirn, r0:r0 + CHUNK, :] = _sigmoid(bac)
            g_c = -acol_ref[...] * _softplus(bac + dtcol_ref[...])
            h3, m3, l3 = _split3(g_c)
            d3 = _dot(cum, jnp.concatenate([h3, m3, l3], axis=-1))
            dcol[dirn, r0:r0 + CHUNK, :] = (d3[:, 0:LANES] + d3[:, LANES:2 * LANES]) + d3[:, 2 * LANES:3 * LANES]
            gtc = gt[0, :, r0:r0 + CHUNK]
            g_r = -arow_ref[...] * _softplus(gtc + dtrow_ref[...])
            h3, m3, l3 = _split3(g_r)
            d3 = _dot(jnp.concatenate([h3, m3, l3], axis=0), cum_t)
            drow[dirn, c] = (d3[0:BA_ROWS] + d3[BA_ROWS:2 * BA_ROWS]) + d3[2 * BA_ROWS:3 * BA_ROWS]

    ri2 = lax.broadcasted_iota(jnp.int32, (CHUNK, LANES), 0)
    ln2 = lax.broadcasted_iota(jnp.int32, (CHUNK, LANES), 1)
    lo = ln2 < HEAD_DIM
    eye_hi = jnp.where(ri2 == ln2 - HEAD_DIM, 1.0, 0.0)
    keep_qk = (lo & (ri2 >= ln2), lo & (ri2 <= ln2))
    keep_m = (lo & (ri2 > ln2), lo & (ri2 < ln2))
    rs2 = lax.broadcasted_iota(jnp.int32, (LANES, LANES), 0)
    cs2 = lax.broadcasted_iota(jnp.int32, (LANES, LANES), 1)
    same_head = (rs2 < HEAD_DIM) == (cs2 < HEAD_DIM)
    lo_row = lax.broadcasted_iota(jnp.int32, (1, LANES), 1) < HEAD_DIM
    zero_b = jnp.zeros((CHUNK, LANES), BF16)

    def chunk_group(cg, carry):
        pairs, chains = [], []
        for j in range(DELTA_UNROLL):
            c = cg * DELTA_UNROLL + j
            for dirn, (qkv, _, _, _) in enumerate(ins):
                cc = c if dirn == 0 else nc - 1 - c
                r0 = pl.multiple_of(cc * CHUNK, CHUNK)
                beta_all = bcol[dirn, pl.ds(r0, CHUNK), :]
                dcol_all = dcol[dirn, pl.ds(r0, CHUNK), :]
                drow_all = drow[dirn, cc]
                for p in range(npair):
                    cols = lambda part: slice(part * A_WIDTH + p * LANES, part * A_WIDTH + (p + 1) * LANES)
                    q2 = qkv[0, pl.ds(r0, CHUNK), cols(0)]
                    k2 = qkv[0, pl.ds(r0, CHUNK), cols(1)]
                    v2 = qkv[0, pl.ds(r0, CHUNK), cols(2)]
                    pair = dict(j=j, dirn=dirn, p=p, r0=r0, k2=k2, v2=v2, qb=q2.astype(BF16), kb=k2.astype(BF16))
                    pairs.append(pair)
                    for e in range(2):
                        lb = A_HEADS * dirn + 2 * p + e
                        lg = G_OFF + lb
                        dr = drow_all[lg:lg + 1, :]
                        chains.append(dict(
                            pair=pair, e=e, dirn=dirn, own=lo if e == 0 else ~lo,
                            beta=jnp.broadcast_to(beta_all[:, lb:lb + 1], (CHUNK, LANES)),
                            dc=jnp.broadcast_to(dcol_all[:, lg:lg + 1], (CHUNK, LANES)),
                            dr2=jnp.concatenate([dr, dr], axis=-1),
                            dl=dr[:, CHUNK - 1:CHUNK] if dirn == 0 else dr[:, 0:1]))
        for x in chains:
            km = jnp.where(x["own"], x["pair"]["kb"], zero_b)
            qm = jnp.where(x["own"], x["pair"]["qb"], zero_b)
            x["gp"] = _dot_nt(jnp.concatenate([km, qm], axis=0), jnp.concatenate([km, km], axis=0))
        pms = []
        for x in chains:
            gam = jnp.exp(x["dc"] - x["dr2"])
            x["e_dc"] = jnp.exp(x["dc"])
            gp = x["gp"]
            pms.append(jnp.where(keep_m[x["dirn"]], -(gp[0:CHUNK] * x["beta"] * gam), eye_hi))
            x["qk"] = jnp.where(keep_qk[x["dirn"]], gp[CHUNK:2 * CHUNK] * SCALE * gam, 0.0)[:, 0:HEAD_DIM].astype(BF16)
            rhs = jnp.concatenate([x["pair"]["v2"] * x["beta"], x["pair"]["k2"] * (x["beta"] * x["e_dc"])], axis=-1)
            x["rhs"] = jnp.concatenate([jnp.zeros((CHUNK, 2 * LANES), BF16), rhs.astype(BF16)], axis=0)
        for s in range(6):
            pmbs = [pm.astype(BF16) for pm in pms]
            mrs = [_dot(pmb[:, 0:HEAD_DIM], pmb) for pmb in pmbs]
            pms = [jnp.where(lo, mr, pm + mr) for pm, mr in zip(pms, mrs)]
        uws = [_dot(pm.astype(BF16), x["rhs"]) for pm, x in zip(pms, chains)]
        per_chunk = 2 * npair
        for j in range(DELTA_UNROLL):
            prs = pairs[j * per_chunk:(j + 1) * per_chunk]
            chs = chains[2 * j * per_chunk:2 * (j + 1) * per_chunk]
            uwj = uws[2 * j * per_chunk:2 * (j + 1) * per_chunk]
            sbs = [st[x["dirn"], x["p"]] for x in prs]
            sbbs = [s_old.astype(BF16) for s_old in sbs]
            wss = [_dot(uw[:, LANES:2 * LANES].astype(BF16), sbbs[n // 2]) for n, uw in enumerate(uwj)]
            qss = [_dot(x["qb"], sbb) for x, sbb in zip(prs, sbbs)]
            vns = [uw[:, 0:LANES] - ws for uw, ws in zip(uwj, wss)]
            o2s = [_dot(x["qk"], vn.astype(BF16)) for x, vn in zip(chs, vns)]
            vss = [vn * jnp.exp(x["dl"] - x["dc"]) for x, vn in zip(chs, vns)]
            for n, x in enumerate(prs):
                a, b2 = chs[2 * n], chs[2 * n + 1]
                tn = _dot_tn(x["kb"], jnp.where(lo, vss[2 * n], vss[2 * n + 1]).astype(BF16))
                cd = jnp.where(lo_row, jnp.exp(a["dl"]), jnp.exp(b2["dl"]))
                st[x["dirn"], x["p"]] = sbs[n] * cd + jnp.where(same_head, tn, 0.0)
                o = jnp.where(lo, o2s[2 * n] + qss[n] * (SCALE * a["e_dc"]),
                              o2s[2 * n + 1] + qss[n] * (SCALE * b2["e_dc"]))
                ins[x["dirn"]][3][0, pl.ds(x["r0"], CHUNK), x["p"] * LANES:(x["p"] + 1) * LANES] = o
        return carry

    lax.fori_loop(0, nc // DELTA_UNROLL, chunk_group, 0)


def _delta(proj, gt, acol, dtcol, arow, dtrow):
    TB = MIX_TB
    b, t, _ = proj.shape
    nb = t // TB
    ba_blk = P_BA // LANES

    def specs(blk_of):
        return [
            pl.BlockSpec((1, TB, QKV_W), lambda bi, i: (bi, blk_of(i), 0)),
            pl.BlockSpec((1, TB, LANES), lambda bi, i: (bi, blk_of(i), ba_blk)),
            pl.BlockSpec((1, BA_ROWS, TB), lambda bi, i: (bi, 0, blk_of(i))),
        ]

    fwd = lambda i: i
    bwd = lambda i: nb - 1 - i
    const = lambda shape: pl.BlockSpec(shape, lambda bi, i: (0, 0))
    return pl.pallas_call(
        functools.partial(_delta_body, nb),
        grid=(b, nb),
        in_specs=specs(fwd) + specs(bwd) + [const((1, LANES)), const((1, LANES)), const((BA_ROWS, 1)), const((BA_ROWS, 1))],
        out_specs=[
            pl.BlockSpec((1, TB, A_WIDTH), lambda bi, i: (bi, i, 0)),
            pl.BlockSpec((1, TB, A_WIDTH), lambda bi, i: (bi, nb - 1 - i, 0)),
        ],
        out_shape=[jax.ShapeDtypeStruct((b, t, A_WIDTH), F32)] * 2,
        scratch_shapes=[
            pltpu.VMEM((2, TB, LANES), F32),
            pltpu.VMEM((2, TB, LANES), F32),
            pltpu.VMEM((2, TB // CHUNK, BA_ROWS, CHUNK), F32),
            pltpu.VMEM((2, A_HEADS // 2, LANES, LANES), F32),
        ],
        compiler_params=_params(("parallel", "arbitrary")),
        name="delta",
    )(proj, proj, gt, proj, proj, gt, acol, dtcol, arow, dtrow)


def _retention_tables_body(z_ref, dmat, tab):
    lg = -_softplus(-z_ref[...])
    ri = lax.broadcasted_iota(jnp.int32, (RCHUNK, RCHUNK), 0)
    ci = lax.broadcasted_iota(jnp.int32, (RCHUNK, RCHUNK), 1)
    diff = (ri - ci).astype(F32)
    pos = ri.astype(F32)
    lo = ci < HEAD_DIM
    for h in range(B_HEADS):
        lgf = lg[0:1, h:h + 1]
        lgb = lg[1:2, h:h + 1]
        dmat[h] = SCALE * (jnp.where(ri >= ci, jnp.exp(diff * lgf), 0.0) + jnp.where(ci >= ri, jnp.exp(-diff * lgb), 0.0))
    for p in range(B_HEADS // 2):
        lgf = jnp.where(lo, lg[0:1, 2 * p:2 * p + 1], lg[0:1, 2 * p + 1:2 * p + 2])
        lgb = jnp.where(lo, lg[1:2, 2 * p:2 * p + 1], lg[1:2, 2 * p + 1:2 * p + 2])
        tab[p, 0] = SCALE * jnp.exp((pos + 1.0) * lgf)
        tab[p, 1] = jnp.exp((RCHUNK - 1.0 - pos) * lgf)
        tab[p, 2] = SCALE * jnp.exp((RCHUNK - pos) * lgb)
        tab[p, 3] = jnp.exp(pos * lgb)
        tab[p, 4] = jnp.exp(RCHUNK * lgf)
        tab[p, 5] = jnp.exp(RCHUNK * lgb)


def _retention_tables(z):
    return pl.pallas_call(
        _retention_tables_body,
        out_shape=[jax.ShapeDtypeStruct((B_HEADS, RCHUNK, RCHUNK), F32),
                   jax.ShapeDtypeStruct((B_HEADS // 2, 6, RCHUNK, LANES), F32)],
        name="retention_tables",
    )(z)


def _retention_body(nb, qf, kf, vf, qb_, kb_, vb_, dmat, tab, of_ref, ob_ref, st):
    TB = MIX_TB
    i = pl.program_id(1)
    nc = TB // RCHUNK
    npair = B_HEADS // 2

    @pl.when(i == 0)
    def _():
        st[...] = jnp.zeros_like(st)

    ri = lax.broadcasted_iota(jnp.int32, (RCHUNK, RCHUNK), 0)
    ci = lax.broadcasted_iota(jnp.int32, (RCHUNK, RCHUNK), 1)
    lo = ci < HEAD_DIM
    same_head = (ri < HEAD_DIM) == lo

    def chunk_step(c, carry):
        rf = pl.multiple_of(c * RCHUNK, RCHUNK)
        rb = pl.multiple_of((nc - 1 - c) * RCHUNK, RCHUNK)
        loads = []
        for p in range(npair):
            cols = slice(p * LANES, (p + 1) * LANES)
            loads.append((qf[0, pl.ds(rf, RCHUNK), cols].astype(BF16), kf[0, pl.ds(rf, RCHUNK), cols].astype(BF16),
                          vf[0, pl.ds(rf, RCHUNK), cols], qb_[0, pl.ds(rb, RCHUNK), cols].astype(BF16),
                          kb_[0, pl.ds(rb, RCHUNK), cols].astype(BF16), vb_[0, pl.ds(rb, RCHUNK), cols]))
        zero = jnp.zeros((RCHUNK, LANES), BF16)
        s_lo = [_dot_nt(x[0], jnp.where(lo, x[1], zero)) for x in loads]
        s_hi = [_dot_nt(x[0], jnp.where(lo, zero, x[1])) for x in loads]
        qsf = [_dot(x[0], st[0, p].astype(BF16)) for p, x in enumerate(loads)]
        qsb = [_dot(x[3], st[1, p].astype(BF16)) for p, x in enumerate(loads)]
        tnf = [_dot_tn(x[1], (x[2] * tab[p, 1]).astype(BF16)) for p, x in enumerate(loads)]
        tnb = [_dot_tn(x[4], (x[5] * tab[p, 3]).astype(BF16)) for p, x in enumerate(loads)]
        intra = []
        for p, x in enumerate(loads):
            vb16 = x[2].astype(BF16)
            pm = jnp.concatenate([(s_lo[p] * dmat[2 * p]).astype(BF16), (s_hi[p] * dmat[2 * p + 1]).astype(BF16)], axis=-1)
            v2 = jnp.concatenate([jnp.where(lo, vb16, zero), jnp.where(lo, zero, vb16)], axis=0)
            intra.append(_dot(pm, v2))
        for p in range(npair):
            cols = slice(p * LANES, (p + 1) * LANES)
            of_ref[0, pl.ds(rf, RCHUNK), cols] = intra[p] + qsf[p] * tab[p, 0]
            ob_ref[0, pl.ds(rb, RCHUNK), cols] = qsb[p] * tab[p, 2]
            st[0, p] = st[0, p] * tab[p, 4] + jnp.where(same_head, tnf[p], 0.0)
            st[1, p] = st[1, p] * tab[p, 5] + jnp.where(same_head, tnb[p], 0.0)
        return carry

    lax.fori_loop(0, nc, chunk_step, 0)


def _retention(proj, dmat, tab):
    TB = MIX_TB
    b, t, _ = proj.shape
    nb = t // TB

    def spec(col, blk_of):
        return pl.BlockSpec((1, TB, B_WIDTH), lambda bi, i: (bi, blk_of(i), col // B_WIDTH))

    fwd = lambda i: i
    bwd = lambda i: nb - 1 - i
    return pl.pallas_call(
        functools.partial(_retention_body, nb),
        grid=(b, nb),
        in_specs=[spec(P_BQ, fwd), spec(P_BK, fwd), spec(P_BV, fwd), spec(P_BQ, bwd), spec(P_BK, bwd), spec(P_BV, bwd),
                  pl.BlockSpec(dmat.shape, lambda bi, i: (0, 0, 0)), pl.BlockSpec(tab.shape, lambda bi, i: (0, 0, 0, 0))],
        out_specs=[
            pl.BlockSpec((1, TB, B_WIDTH), lambda bi, i: (bi, i, 0)),
            pl.BlockSpec((1, TB, B_WIDTH), lambda bi, i: (bi, nb - 1 - i, 0)),
        ],
        out_shape=[jax.ShapeDtypeStruct((b, t, B_WIDTH), F32)] * 2,
        scratch_shapes=[
            pltpu.VMEM((2, B_HEADS // 2, LANES, LANES), F32),
        ],
        compiler_params=_params(("parallel", "arbitrary")),
        name="retention",
    )(proj, proj, proj, proj, proj, proj, dmat, tab)


def _attention_body(nq, q_ref, km, kp, kn, vm, vp, vn, gate_ref, sink_ref, o_ref, kx, vx, bias):
    TB = MIX_TB
    i = pl.program_id(1)
    nqb = TB // QBLOCK
    lo_k = lax.broadcasted_iota(jnp.int32, (TB + 2 * QBLOCK, LANES), 1) < HEAD_DIM
    for src, dst in (((kp, km, kn), kx), ((vp, vm, vn), vx)):
        full = jnp.concatenate([src[0][0], src[1][0], src[2][0]], axis=0)
        swapped = pltpu.roll(full, HEAD_DIM, 1)
        dst[0] = jnp.where(lo_k, full, swapped).astype(BF16)
        dst[1] = jnp.where(lo_k, swapped, full).astype(BF16)

    qi = lax.broadcasted_iota(jnp.int32, (QBLOCK, 3 * QBLOCK), 0)
    kj = lax.broadcasted_iota(jnp.int32, (QBLOCK, 3 * QBLOCK), 1)
    rel = jnp.abs(kj - QBLOCK - qi)
    relf = rel.astype(F32)
    for hq in range(C_HEADS):
        slope = 2.0 ** (-8.0 * (hq + 1) / C_HEADS)
        bias[hq] = jnp.where(rel <= WINDOW, -slope * relf, -jnp.inf)
    first_lo = jnp.where(i == 0, QBLOCK, 0)
    last_hi = jnp.where(i == nq - 1, 2 * QBLOCK, 3 * QBLOCK)
    sink_all = sink_ref[...]
    lo_q = lax.broadcasted_iota(jnp.int32, (QBLOCK, LANES), 1) < HEAD_DIM
    zero = jnp.zeros((QBLOCK, LANES), BF16)

    items = [(qb, g) for qb in range(nqb) for g in range(C_KV_HEADS)]

    def scores(item):
        qb, g = item
        q = (q_ref[0, qb * QBLOCK:(qb + 1) * QBLOCK, g * LANES:(g + 1) * LANES] * SCALE).astype(BF16)
        qm = jnp.concatenate([jnp.where(lo_q, q, zero), jnp.where(lo_q, zero, q)], axis=0)
        return _dot_nt(qm, kx[g, qb * QBLOCK:(qb + 3) * QBLOCK, :])

    for i0 in range(0, len(items), ATTN_GROUP):
        group = items[i0:i0 + ATTN_GROUP]
        s_all = [scores(item) for item in group]
        heads = [(n, j) for n in range(len(group)) for j in range(C_GROUP)]
        ss = []
        for n, j in heads:
            qb, g = group[n]
            s = s_all[n][j * QBLOCK:(j + 1) * QBLOCK] + bias[g * C_GROUP + j]
            if qb == 0:
                s = jnp.where(kj >= first_lo, s, -jnp.inf)
            if qb == nqb - 1:
                s = jnp.where(kj < last_hi, s, -jnp.inf)
            ss.append(s)
        sinks = [sink_all[:, group[n][1] * C_GROUP + j:group[n][1] * C_GROUP + j + 1] for n, j in heads]
        ms = [jnp.maximum(jnp.max(s, axis=-1, keepdims=True), sink) for s, sink in zip(ss, sinks)]
        ps = [jnp.exp(s - m) for s, m in zip(ss, ms)]
        dens = [jnp.sum(p, axis=-1, keepdims=True) + jnp.exp(sink - m) for p, sink, m in zip(ps, sinks, ms)]
        pbs = [p.astype(BF16) for p in ps]
        o2s = [_dot(jnp.concatenate(pbs[C_GROUP * n:C_GROUP * (n + 1)], axis=0), vx[g, qb * QBLOCK:(qb + 3) * QBLOCK, :])
               for n, (qb, g) in enumerate(group)]
        for n, (qb, g) in enumerate(group):
            o = jnp.where(lo_q, o2s[n][0:QBLOCK] * (1.0 / dens[2 * n]), o2s[n][QBLOCK:2 * QBLOCK] * (1.0 / dens[2 * n + 1]))
            rows = slice(qb * QBLOCK, (qb + 1) * QBLOCK)
            cols = slice(g * LANES, (g + 1) * LANES)
            o_ref[0, rows, cols] = o * _silu(gate_ref[0, rows, cols])


def _attention(proj, sink):
    TB = MIX_TB
    b, t, _ = proj.shape
    nq = t // TB
    per = TB // QBLOCK
    last = t // QBLOCK - 1

    def halo(col):
        cb = col // C_KV_WIDTH
        return [
            pl.BlockSpec((1, TB, C_KV_WIDTH), lambda bi, i: (bi, i, cb)),
            pl.BlockSpec((1, QBLOCK, C_KV_WIDTH), lambda bi, i: (bi, jnp.maximum(i * per - 1, 0), cb)),
            pl.BlockSpec((1, QBLOCK, C_KV_WIDTH), lambda bi, i: (bi, jnp.minimum((i + 1) * per, last), cb)),
        ]

    return pl.pallas_call(
        functools.partial(_attention_body, nq),
        grid=(b, nq),
        in_specs=[pl.BlockSpec((1, TB, C_WIDTH), lambda bi, i: (bi, i, P_CQ // C_WIDTH))] + halo(P_CK) + halo(P_CV) + [
            pl.BlockSpec((1, TB, C_WIDTH), lambda bi, i: (bi, i, P_CG // C_WIDTH)),
            pl.BlockSpec((1, LANES), lambda bi, i: (0, 0))],
        out_specs=pl.BlockSpec((1, TB, C_WIDTH), lambda bi, i: (bi, i, 0)),
        out_shape=jax.ShapeDtypeStruct((b, t, C_WIDTH), F32),
        scratch_shapes=[
            pltpu.VMEM((C_KV_HEADS, TB + 2 * QBLOCK, LANES), BF16),
            pltpu.VMEM((C_KV_HEADS, TB + 2 * QBLOCK, LANES), BF16),
            pltpu.VMEM((C_HEADS, QBLOCK, 3 * QBLOCK), F32),
        ],
        compiler_params=_params(("parallel", "parallel")),
        name="attention",
    )(proj, proj, proj, proj, proj, proj, proj, proj, sink)


def _out_proj_body(final, h_ref, af, ab, ag, bf, bb, bg, mc, p_ref, dng, rng, wo, wple, wpg, fg, o_ref):
    lo_lane = _pair_masks(TB)
    inv = 1.0 / HEAD_DIM

    def head_stat(x):
        outs = []
        for p in range(x.shape[-1] // LANES):
            xb = x[:, p * LANES:(p + 1) * LANES]
            s_lo = jnp.sum(jnp.where(lo_lane, xb, 0.0), axis=-1, keepdims=True)
            s_hi = jnp.sum(jnp.where(lo_lane, 0.0, xb), axis=-1, keepdims=True)
            outs.append(jnp.where(lo_lane, s_lo, s_hi) * inv)
        return jnp.concatenate(outs, axis=-1)

    oa = af[0] + ab[0]
    ya = oa * lax.rsqrt(head_stat(oa * oa) + EPS) * dng[...] * _silu(ag[0])
    ob = bf[0] + bb[0]
    xc = ob - head_stat(ob)
    yb = xc * lax.rsqrt(head_stat(xc * xc) + EPS) * rng[...] * _silu(bg[0])
    mix = jnp.concatenate([ya, yb, mc[0]], axis=-1).astype(BF16)
    h1 = h_ref[0] + _dot(mix, wo[...])
    h2 = h1 + _dot(p_ref[0, 0].astype(BF16), wple[...]) * _sigmoid(_dot(h1.astype(BF16), wpg[...]))
    if final:
        ms = jnp.mean(h2 * h2, axis=-1, keepdims=True)
        h2 = h2 * lax.rsqrt(ms + EPS) * fg[...]
    o_ref[0] = h2


def _out_proj(final, layer, h, oaf, oab, obf, obb, mixc, proj, p, dng, rng, wo, wple, wpg, fg):
    b, t, _ = h.shape
    row = lambda w: pl.BlockSpec((1, TB, w), lambda bi, i: (bi, i, 0))
    col = lambda w, c: pl.BlockSpec((1, TB, w), lambda bi, i: (bi, i, c // w))
    const = lambda shape: pl.BlockSpec(shape, lambda bi, i: (0, 0))
    return pl.pallas_call(
        functools.partial(_out_proj_body, final),
        grid=(b, t // TB),
        in_specs=[row(D_MODEL), row(A_WIDTH), row(A_WIDTH), col(A_WIDTH, P_AG), row(B_WIDTH), row(B_WIDTH), col(B_WIDTH, P_BG),
                  row(C_WIDTH), pl.BlockSpec((1, 1, TB, PLE_DIM), lambda bi, i: (layer, bi, i, 0)),
                  const((1, A_WIDTH)), const((1, B_WIDTH)), const((D_MODEL, D_MODEL)), const((PLE_DIM, D_MODEL)),
                  const((D_MODEL, D_MODEL)), const((1, D_MODEL))],
        out_specs=row(D_MODEL),
        out_shape=jax.ShapeDtypeStruct((b, t, D_MODEL), F32),
        compiler_params=_params(("parallel", "parallel")),
        name="out_proj",
    )(h, oaf, oab, proj, obf, obb, proj, mixc, p, dng, rng, wo, wple, wpg, fg)


def _layer_weights(i, w_in, w_out, norm_g, conv_w, dn_a_log, dn_dt_bias, dn_norm_g, ret_decay_z, ret_norm_g, attn_sink,
                   w_ple, w_pg):
    w = w_in[i]
    w_p = jnp.concatenate([w[:, :ORIG_BA], w[:, ORIG_BA + N_BA:], w[:, ORIG_BA:ORIG_BA + N_BA],
                           jnp.zeros((D_MODEL, LANES - N_BA), F32)], axis=-1).astype(BF16)
    w_ba_t = jnp.pad(w[:, ORIG_BA:ORIG_BA + N_BA].T, ((0, BA_ROWS - N_BA), (0, 0))).astype(BF16)
    a = jnp.exp(dn_a_log[i].astype(F32)).reshape(-1)
    dt = dn_dt_bias[i].astype(F32).reshape(-1)
    place = lambda x, n: jnp.zeros((n,), F32).at[G_OFF:G_OFF + 2 * A_HEADS].set(x)
    dmat, tab = _retention_tables(jnp.pad(ret_decay_z[i].astype(F32), ((0, SUBLANES - 2), (0, LANES - B_HEADS))))
    return dict(
        dmat=dmat, tab=tab,
        norm_g=norm_g[i].reshape(1, D_MODEL).astype(F32), w_p=w_p, w_ba_t=w_ba_t,
        cw=jnp.pad(conv_w[i].astype(F32), ((0, SUBLANES - CONV_K), (0, 0))),
        acol=place(a, LANES).reshape(1, LANES), dtcol=place(dt, LANES).reshape(1, LANES),
        arow=place(a, BA_ROWS).reshape(BA_ROWS, 1), dtrow=place(dt, BA_ROWS).reshape(BA_ROWS, 1),
        sink=jnp.pad(attn_sink[i].astype(F32), (0, LANES - C_HEADS)).reshape(1, LANES),
        dng=jnp.tile(dn_norm_g[i].astype(F32), A_HEADS).reshape(1, A_WIDTH),
        rng=ret_norm_g[i].astype(F32).reshape(1, B_WIDTH),
        wo=w_out[i].astype(BF16), wple=w_ple[i].astype(BF16), wpg=w_pg[i].astype(BF16),
    )


def _trunk(x, p, layers, final_g):
    h = x
    fg = final_g.reshape(1, D_MODEL).astype(F32)
    for i, lw in enumerate(layers):
        proj, gt = _in_proj(h, lw["norm_g"], lw["w_p"], lw["w_ba_t"], lw["cw"])
        oaf, oab = _delta(proj, gt, lw["acol"], lw["dtcol"], lw["arow"], lw["dtrow"])
        obf, obb = _retention(proj, lw["dmat"], lw["tab"])
        mixc = _attention(proj, lw["sink"])
        h = _out_proj(i == len(layers) - 1, i, h, oaf, oab, obf, obb, mixc, proj, p, lw["dng"], lw["rng"],
                      lw["wo"], lw["wple"], lw["wpg"], fg)
    return h


def kernel(x_prompt, x_sample, p_prompt, p_sample, w_in, w_out, norm_g, conv_w, dn_a_log, dn_dt_bias, dn_norm_g,
           ret_decay_z, ret_norm_g, attn_sink, w_ple, w_pg, final_g):
    depth = w_in.shape[0]
    layers = [_layer_weights(i, w_in, w_out, norm_g, conv_w, dn_a_log, dn_dt_bias, dn_norm_g, ret_decay_z, ret_norm_g,
                             attn_sink, w_ple, w_pg) for i in range(depth)]
    y_prompt = _trunk(x_prompt, p_prompt, layers, final_g)
    y_sample = _trunk(x_sample, p_sample, layers, final_g)
    return (y_prompt, y_sample)
```

```python
import functools

import jax
import jax.numpy as jnp
from jax import lax
from jax.experimental import pallas as pl
from jax.experimental.pallas import tpu as pltpu

F32 = jnp.float32
BF16 = jnp.bfloat16

D_MODEL = 1024
HEAD_DIM = 64
A_HEADS = 6
A_WIDTH = A_HEADS * HEAD_DIM
B_HEADS = 6
B_WIDTH = B_HEADS * HEAD_DIM
C_HEADS = 4
C_KV_HEADS = 2
C_GROUP = C_HEADS // C_KV_HEADS
C_WIDTH = C_HEADS * HEAD_DIM
C_KV_WIDTH = C_KV_HEADS * HEAD_DIM
CONV_K = 5
WINDOW = 128
PLE_DIM = 256
EPS = 1e-6
SCALE = HEAD_DIM ** -0.5

LANES = 128
SUBLANES = 8
VMEM_LIMIT = 48 * 1024 * 1024

QKV_W = 3 * A_WIDTH
P_AG = QKV_W
P_BQ = P_AG + A_WIDTH
P_BK = P_BQ + B_WIDTH
P_BV = P_BK + B_WIDTH
P_BG = P_BV + B_WIDTH
P_CQ = P_BG + B_WIDTH
P_CK = P_CQ + C_WIDTH
P_CV = P_CK + C_KV_WIDTH
P_CG = P_CV + C_KV_WIDTH
P_BA = P_CG + C_WIDTH
PROJ_W = P_BA + LANES
ORIG_BA = QKV_W + A_WIDTH
N_BA = 4 * A_HEADS
BA_ROWS = 32
G_OFF = 2 * A_HEADS

CHUNK = 64
RCHUNK = 128
QBLOCK = 128
TB = 512
MIX_TB = 1024
DELTA_UNROLL = 2
ATTN_GROUP = 4

NT_DIMS = (((1,), (1,)), ((), ()))
TN_DIMS = (((0,), (0,)), ((), ()))


def _dot(a, b):
    return jnp.dot(a, b, preferred_element_type=F32)


def _dot_nt(a, b):
    return lax.dot_general(a, b, NT_DIMS, preferred_element_type=F32)


def _dot_tn(a, b):
    return lax.dot_general(a, b, TN_DIMS, preferred_element_type=F32)


def _sigmoid(x):
    return 0.5 * jnp.tanh(0.5 * x) + 0.5


def _silu(x):
    return x * _sigmoid(x)


def _softplus(x):
    return jnp.maximum(x, 0.0) + jnp.log1p(jnp.exp(-jnp.abs(x)))


def _split3(x):
    hi = x.astype(BF16)
    r1 = x - hi.astype(F32)
    mid = r1.astype(BF16)
    lo = (r1 - mid.astype(F32)).astype(BF16)
    return hi, mid, lo


def _pair_masks(rows):
    lane = lax.broadcasted_iota(jnp.int32, (rows, LANES), 1)
    return lane < HEAD_DIM


def _params(sem, flags=None):
    return pltpu.CompilerParams(dimension_semantics=sem, vmem_limit_bytes=VMEM_LIMIT, flags=flags)


def _in_proj_body(nb, h_ref, hp_ref, hn_ref, g_ref, w_ref, wba_ref, cw_ref, proj_ref, gt_ref, bqkv_ref, xs):
    i = pl.program_id(1)
    x = jnp.concatenate([hp_ref[0], h_ref[0], hn_ref[0]], axis=0)
    ms = jnp.mean(x * x, axis=-1, keepdims=True)
    xn_ext = (x * lax.rsqrt(ms + EPS) * g_ref[...]).astype(BF16)
    xn = xn_ext[SUBLANES:TB + SUBLANES]
    row = lax.broadcasted_iota(jnp.int32, (TB + 2 * SUBLANES, 1), 0)
    inside = ((row >= SUBLANES) | (i > 0)) & ((row < TB + SUBLANES) | (i < nb - 1))
    lo_lane = _pair_masks(CHUNK)
    cw = cw_ref[...]
    base = SUBLANES - CONV_K // 2
    for c0 in range(0, QKV_W, 512):
        c1 = min(c0 + 512, QKV_W)
        xs[:, c0:c1] = jnp.where(inside, _dot(xn_ext, w_ref[:, c0:c1]), 0.0)
    for c0 in range(QKV_W, PROJ_W, 512):
        c1 = min(c0 + 512, PROJ_W)
        val = _dot(xn, w_ref[:, c0:c1])
        proj_ref[0, :, c0:c1] = val
        s0, s1 = max(c0, P_BQ), min(c1, P_BG)
        if s0 < s1:
            bqkv_ref[0, :, s0 - P_BQ:s1 - P_BQ] = val[:, s0 - c0:s1 - c0].astype(BF16)
    gt_ref[0] = _dot_nt(wba_ref[...], xn)
    for part in range(3):
        c0 = part * A_WIDTH
        for r0 in range(0, TB, CHUNK):
            win = xs[r0:r0 + CHUNK + 2 * SUBLANES, c0:c0 + A_WIDTH]
            acc = None
            for k in range(CONV_K):
                shifted = win if k == CONV_K // 2 else pltpu.roll(win, (CONV_K // 2 - k) % (CHUNK + 2 * SUBLANES), 0)
                term = cw[k:k + 1, c0:c0 + A_WIDTH] * shifted[SUBLANES:SUBLANES + CHUNK]
                acc = term if acc is None else acc + term
            y = _silu(acc)
            if part < 2:
                for p in range(A_WIDTH // LANES):
                    y2 = y[:, p * LANES:(p + 1) * LANES]
                    sq = y2 * y2
                    s_lo = jnp.sum(jnp.where(lo_lane, sq, 0.0), axis=-1, keepdims=True)
                    s_hi = jnp.sum(jnp.where(lo_lane, 0.0, sq), axis=-1, keepdims=True)
                    r = jnp.where(lo_lane, lax.rsqrt(s_lo + EPS), lax.rsqrt(s_hi + EPS))
                    proj_ref[0, r0:r0 + CHUNK, c0 + p * LANES:c0 + (p + 1) * LANES] = y2 * r
            else:
                proj_ref[0, r0:r0 + CHUNK, c0:c0 + A_WIDTH] = y


def _in_proj(h, norm_g, w_p, w_ba_t, cw):
    b, t, _ = h.shape
    nb = t // TB
    hb = TB // SUBLANES
    last_h = t // SUBLANES - 1
    const = lambda shape: pl.BlockSpec(shape, lambda bi, i: (0, 0))
    return pl.pallas_call(
        functools.partial(_in_proj_body, nb),
        grid=(b, nb),
        in_specs=[
            pl.BlockSpec((1, TB, D_MODEL), lambda bi, i: (bi, i, 0)),
            pl.BlockSpec((1, SUBLANES, D_MODEL), lambda bi, i: (bi, jnp.maximum(i * hb - 1, 0), 0)),
            pl.BlockSpec((1, SUBLANES, D_MODEL), lambda bi, i: (bi, jnp.minimum((i + 1) * hb, last_h), 0)),
            const((1, D_MODEL)), const((D_MODEL, PROJ_W)), const((BA_ROWS, D_MODEL)), const((SUBLANES, QKV_W)),
        ],
        out_specs=[
            pl.BlockSpec((1, TB, PROJ_W), lambda bi, i: (bi, i, 0)),
            pl.BlockSpec((1, BA_ROWS, TB), lambda bi, i: (bi, 0, i)),
            pl.BlockSpec((1, TB, 3 * B_WIDTH), lambda bi, i: (bi, i, 0)),
        ],
        out_shape=[jax.ShapeDtypeStruct((b, t, PROJ_W), F32), jax.ShapeDtypeStruct((b, BA_ROWS, t), F32),
                   jax.ShapeDtypeStruct((b, t, 3 * B_WIDTH), BF16)],
        scratch_shapes=[pltpu.VMEM((TB + 2 * SUBLANES, QKV_W), F32)],
        compiler_params=_params(("parallel", "parallel")),
        name="in_proj",
    )(h, h, h, norm_g, w_p, w_ba_t, cw)


def _delta_body(nb, *refs):
    TB = MIX_TB
    (qkv_f, ba_f, gt_f, qkv_b, ba_b, gt_b, acol_ref, dtcol_ref, arow_ref, dtrow_ref,
     of_ref, ob_ref, bcol, dcol, drow, st) = refs
    i = pl.program_id(1)
    nc = TB // CHUNK
    npair = A_HEADS // 2

    @pl.when(i == 0)
    def _():
        st[...] = jnp.zeros_like(st)

    ri = lax.broadcasted_iota(jnp.int32, (CHUNK, CHUNK), 0)
    ci = lax.broadcasted_iota(jnp.int32, (CHUNK, CHUNK), 1)
    incl = (ri >= ci, ri <= ci)

    ins = ((qkv_f, ba_f, gt_f, of_ref), (qkv_b, ba_b, gt_b, ob_ref))
    for dirn, (_, ba, gt, _) in enumerate(ins):
        cum = incl[dirn].astype(BF16)
        cum_t = incl[1 - dirn].astype(BF16)
        for c in range(nc):
            r0 = c * CHUNK
            bac = ba[0, r0:r0 + CHUNK, :]
            bcol[dirn, r0:r0 + CHUNK, :] = _sigmoid(bac)
            g_c = -acol_ref[...] * _softplus(bac + dtcol_ref[...])
            h3, m3, l3 = _split3(g_c)
            d3 = _dot(cum, jnp.concatenate([h3, m3, l3], axis=-1))
            dcol[dirn, r0:r0 + CHUNK, :] = (d3[:, 0:LANES] + d3[:, LANES:2 * LANES]) + d3[:, 2 * LANES:3 * LANES]
            gtc = gt[0, :, r0:r0 + CHUNK]
            g_r = -arow_ref[...] * _softplus(gtc + dtrow_ref[...])
            h3, m3, l3 = _split3(g_r)
            d3 = _dot(jnp.concatenate([h3, m3, l3], axis=0), cum_t)
            drow[dirn, c] = (d3[0:BA_ROWS] + d3[BA_ROWS:2 * BA_ROWS]) + d3[2 * BA_ROWS:3 * BA_ROWS]

    ri2 = lax.broadcasted_iota(jnp.int32, (CHUNK, LANES), 0)
    ln2 = lax.broadcasted_iota(jnp.int32, (CHUNK, LANES), 1)
    lo = ln2 < HEAD_DIM
    eye_hi = jnp.where(ri2 == ln2 - HEAD_DIM, 1.0, 0.0)
    keep_qk = (lo & (ri2 >= ln2), lo & (ri2 <= ln2))
    keep_m = (lo & (ri2 > ln2), lo & (ri2 < ln2))
    rs2 = lax.broadcasted_iota(jnp.int32, (LANES, LANES), 0)
    cs2 = lax.broadcasted_iota(jnp.int32, (LANES, LANES), 1)
    same_head = (rs2 < HEAD_DIM) == (cs2 < HEAD_DIM)
    lo_row = lax.broadcasted_iota(jnp.int32, (1, LANES), 1) < HEAD_DIM
    zero_b = jnp.zeros((CHUNK, LANES), BF16)

    def chunk_group(cg, carry):
        pairs, chains = [], []
        for j in range(DELTA_UNROLL):
            c = cg * DELTA_UNROLL + j
            for dirn, (qkv, _, _, _) in enumerate(ins):
                cc = c if dirn == 0 else nc - 1 - c
                r0 = pl.multiple_of(cc * CHUNK, CHUNK)
                beta_all = bcol[dirn, pl.ds(r0, CHUNK), :]
                dcol_all = dcol[dirn, pl.ds(r0, CHUNK), :]
                drow_all = drow[dirn, cc]
                for p in range(npair):
                    cols = lambda part: slice(part * A_WIDTH + p * LANES, part * A_WIDTH + (p + 1) * LANES)
                    q2 = qkv[0, pl.ds(r0, CHUNK), cols(0)]
                    k2 = qkv[0, pl.ds(r0, CHUNK), cols(1)]
                    v2 = qkv[0, pl.ds(r0, CHUNK), cols(2)]
                    pair = dict(j=j, dirn=dirn, p=p, r0=r0, k2=k2, v2=v2, qb=q2.astype(BF16), kb=k2.astype(BF16))
                    pairs.append(pair)
                    for e in range(2):
                        lb = A_HEADS * dirn + 2 * p + e
                        lg = G_OFF + lb
                        dr = drow_all[lg:lg + 1, :]
                        chains.append(dict(
                            pair=pair, e=e, dirn=dirn, own=lo if e == 0 else ~lo,
                            beta=jnp.broadcast_to(beta_all[:, lb:lb + 1], (CHUNK, LANES)),
                            dc=jnp.broadcast_to(dcol_all[:, lg:lg + 1], (CHUNK, LANES)),
                            dr2=jnp.concatenate([dr, dr], axis=-1),
                            dl=dr[:, CHUNK - 1:CHUNK] if dirn == 0 else dr[:, 0:1]))
        for x in chains:
            km = jnp.where(x["own"], x["pair"]["kb"], zero_b)
            qm = jnp.where(x["own"], x["pair"]["qb"], zero_b)
            x["gp"] = _dot_nt(jnp.concatenate([km, qm], axis=0), jnp.concatenate([km, km], axis=0))
        pms = []
        for x in chains:
            gam = jnp.exp(x["dc"] - x["dr2"])
            x["e_dc"] = jnp.exp(x["dc"])
            gp = x["gp"]
            pms.append(jnp.where(keep_m[x["dirn"]], -(gp[0:CHUNK] * x["beta"] * gam), eye_hi))
            x["qk"] = jnp.where(keep_qk[x["dirn"]], gp[CHUNK:2 * CHUNK] * SCALE * gam, 0.0)[:, 0:HEAD_DIM].astype(BF16)
            rhs = jnp.concatenate([x["pair"]["v2"] * x["beta"], x["pair"]["k2"] * (x["beta"] * x["e_dc"])], axis=-1)
            x["rhs"] = jnp.concatenate([jnp.zeros((CHUNK, 2 * LANES), BF16), rhs.astype(BF16)], axis=0)
        for s in range(6):
            pmbs = [pm.astype(BF16) for pm in pms]
            mrs = [_dot(pmb[:, 0:HEAD_DIM], pmb) for pmb in pmbs]
            pms = [jnp.where(lo, mr, pm + mr) for pm, mr in zip(pms, mrs)]
        uws = [_dot(pm.astype(BF16), x["rhs"]) for pm, x in zip(pms, chains)]
        per_chunk = 2 * npair
        for j in range(DELTA_UNROLL):
            prs = pairs[j * per_chunk:(j + 1) * per_chunk]
            chs = chains[2 * j * per_chunk:2 * (j + 1) * per_chunk]
            uwj = uws[2 * j * per_chunk:2 * (j + 1) * per_chunk]
            sbs = [st[x["dirn"], x["p"]] for x in prs]
            sbbs = [s_old.astype(BF16) for s_old in sbs]
            wss = [_dot(uw[:, LANES:2 * LANES].astype(BF16), sbbs[n // 2]) for n, uw in enumerate(uwj)]
            qss = [_dot(x["qb"], sbb) for x, sbb in zip(prs, sbbs)]
            vns = [uw[:, 0:LANES] - ws for uw, ws in zip(uwj, wss)]
            o2s = [_dot(x["qk"], vn.astype(BF16)) for x, vn in zip(chs, vns)]
            vss = [vn * jnp.exp(x["dl"] - x["dc"]) for x, vn in zip(chs, vns)]
            for n, x in enumerate(prs):
                a, b2 = chs[2 * n], chs[2 * n + 1]
                tn = _dot_tn(x["kb"], jnp.where(lo, vss[2 * n], vss[2 * n + 1]).astype(BF16))
                cd = jnp.where(lo_row, jnp.exp(a["dl"]), jnp.exp(b2["dl"]))
                st[x["dirn"], x["p"]] = sbs[n] * cd + jnp.where(same_head, tn, 0.0)
                o = jnp.where(lo, o2s[2 * n] + qss[n] * (SCALE * a["e_dc"]),
                              o2s[2 * n + 1] + qss[n] * (SCALE * b2["e_dc"]))
                ins[x["dirn"]][3][0, pl.ds(x["r0"], CHUNK), x["p"] * LANES:(x["p"] + 1) * LANES] = o
        return carry

    lax.fori_loop(0, nc // DELTA_UNROLL, chunk_group, 0)


def _delta(proj, gt, acol, dtcol, arow, dtrow):
    TB = MIX_TB
    b, t, _ = proj.shape
    nb = t // TB
    ba_blk = P_BA // LANES

    def specs(blk_of):
        return [
            pl.BlockSpec((1, TB, QKV_W), lambda bi, i: (bi, blk_of(i), 0)),
            pl.BlockSpec((1, TB, LANES), lambda bi, i: (bi, blk_of(i), ba_blk)),
            pl.BlockSpec((1, BA_ROWS, TB), lambda bi, i: (bi, 0, blk_of(i))),
        ]

    fwd = lambda i: i
    bwd = lambda i: nb - 1 - i
    const = lambda shape: pl.BlockSpec(shape, lambda bi, i: (0, 0))
    return pl.pallas_call(
        functools.partial(_delta_body, nb),
        grid=(b, nb),
        in_specs=specs(fwd) + specs(bwd) + [const((1, LANES)), const((1, LANES)), const((BA_ROWS, 1)), const((BA_ROWS, 1))],
        out_specs=[
            pl.BlockSpec((1, TB, A_WIDTH), lambda bi, i: (bi, i, 0)),
            pl.BlockSpec((1, TB, A_WIDTH), lambda bi, i: (bi, nb - 1 - i, 0)),
        ],
        out_shape=[jax.ShapeDtypeStruct((b, t, A_WIDTH), F32)] * 2,
        scratch_shapes=[
            pltpu.VMEM((2, TB, LANES), F32),
            pltpu.VMEM((2, TB, LANES), F32),
            pltpu.VMEM((2, TB // CHUNK, BA_ROWS, CHUNK), F32),
            pltpu.VMEM((2, A_HEADS // 2, LANES, LANES), F32),
        ],
        compiler_params=_params(("parallel", "arbitrary")),
        name="delta",
    )(proj, proj, gt, proj, proj, gt, acol, dtcol, arow, dtrow)


def _retention_tables_body(z_ref, dmat, tab):
    lg = -_softplus(-z_ref[...])
    ri = lax.broadcasted_iota(jnp.int32, (RCHUNK, RCHUNK), 0)
    ci = lax.broadcasted_iota(jnp.int32, (RCHUNK, RCHUNK), 1)
    diff = (ri - ci).astype(F32)
    pos = ri.astype(F32)
    lo = ci < HEAD_DIM
    for h in range(B_HEADS):
        lgf = lg[0:1, h:h + 1]
        lgb = lg[1:2, h:h + 1]
        dmat[h] = SCALE * (jnp.where(ri >= ci, jnp.exp(diff * lgf), 0.0) + jnp.where(ci >= ri, jnp.exp(-diff * lgb), 0.0))
    for p in range(B_HEADS // 2):
        lgf = jnp.where(lo, lg[0:1, 2 * p:2 * p + 1], lg[0:1, 2 * p + 1:2 * p + 2])
        lgb = jnp.where(lo, lg[1:2, 2 * p:2 * p + 1], lg[1:2, 2 * p + 1:2 * p + 2])
        tab[p, 0] = SCALE * jnp.exp((pos + 1.0) * lgf)
        tab[p, 1] = jnp.exp((RCHUNK - 1.0 - pos) * lgf)
        tab[p, 2] = SCALE * jnp.exp((RCHUNK - pos) * lgb)
        tab[p, 3] = jnp.exp(pos * lgb)
        tab[p, 4] = jnp.exp(RCHUNK * lgf)
        tab[p, 5] = jnp.exp(RCHUNK * lgb)


def _retention_tables(z):
    return pl.pallas_call(
        _retention_tables_body,
        out_shape=[jax.ShapeDtypeStruct((B_HEADS, RCHUNK, RCHUNK), F32),
                   jax.ShapeDtypeStruct((B_HEADS // 2, 6, RCHUNK, LANES), F32)],
        name="retention_tables",
    )(z)


def _retention_body(nb, qf, kf, vf, qb_, kb_, vb_, dmat, tab, of_ref, ob_ref, st):
    TB = MIX_TB
    i = pl.program_id(1)
    nc = TB // RCHUNK
    npair = B_HEADS // 2

    @pl.when(i == 0)
    def _():
        st[...] = jnp.zeros_like(st)

    ri = lax.broadcasted_iota(jnp.int32, (RCHUNK, RCHUNK), 0)
    ci = lax.broadcasted_iota(jnp.int32, (RCHUNK, RCHUNK), 1)
    lo = ci < HEAD_DIM
    same_head = (ri < HEAD_DIM) == lo

    def chunk_step(c, carry):
        rf = pl.multiple_of(c * RCHUNK, RCHUNK)
        rb = pl.multiple_of((nc - 1 - c) * RCHUNK, RCHUNK)
        loads = []
        for p in range(npair):
            cols = slice(p * LANES, (p + 1) * LANES)
            loads.append((qf[0, pl.ds(rf, RCHUNK), cols].astype(BF16), kf[0, pl.ds(rf, RCHUNK), cols].astype(BF16),
                          vf[0, pl.ds(rf, RCHUNK), cols], qb_[0, pl.ds(rb, RCHUNK), cols].astype(BF16),
                          kb_[0, pl.ds(rb, RCHUNK), cols].astype(BF16), vb_[0, pl.ds(rb, RCHUNK), cols]))
        zero = jnp.zeros((RCHUNK, LANES), BF16)
        s_lo = [_dot_nt(x[0], jnp.where(lo, x[1], zero)) for x in loads]
        s_hi = [_dot_nt(x[0], jnp.where(lo, zero, x[1])) for x in loads]
        qsf = [_dot(x[0], st[0, p].astype(BF16)) for p, x in enumerate(loads)]
        qsb = [_dot(x[3], st[1, p].astype(BF16)) for p, x in enumerate(loads)]
        tnf = [_dot_tn(x[1], (x[2] * tab[p, 1]).astype(BF16)) for p, x in enumerate(loads)]
        tnb = [_dot_tn(x[4], (x[5] * tab[p, 3]).astype(BF16)) for p, x in enumerate(loads)]
        intra = []
        for p, x in enumerate(loads):
            vb16 = x[2].astype(BF16)
            pm = jnp.concatenate([(s_lo[p] * dmat[2 * p]).astype(BF16), (s_hi[p] * dmat[2 * p + 1]).astype(BF16)], axis=-1)
            v2 = jnp.concatenate([jnp.where(lo, vb16, zero), jnp.where(lo, zero, vb16)], axis=0)
            intra.append(_dot(pm, v2))
        for p in range(npair):
            cols = slice(p * LANES, (p + 1) * LANES)
            of_ref[0, pl.ds(rf, RCHUNK), cols] = intra[p] + qsf[p] * tab[p, 0]
            ob_ref[0, pl.ds(rb, RCHUNK), cols] = qsb[p] * tab[p, 2]
            st[0, p] = st[0, p] * tab[p, 4] + jnp.where(same_head, tnf[p], 0.0)
            st[1, p] = st[1, p] * tab[p, 5] + jnp.where(same_head, tnb[p], 0.0)
        return carry

    lax.fori_loop(0, nc, chunk_step, 0)


def _retention(bqkv, dmat, tab):
    TB = MIX_TB
    b, t, _ = bqkv.shape
    nb = t // TB

    def spec(part, blk_of):
        return pl.BlockSpec((1, TB, B_WIDTH), lambda bi, i: (bi, blk_of(i), part))

    fwd = lambda i: i
    bwd = lambda i: nb - 1 - i
    return pl.pallas_call(
        functools.partial(_retention_body, nb),
        grid=(b, nb),
        in_specs=[spec(0, fwd), spec(1, fwd), spec(2, fwd), spec(0, bwd), spec(1, bwd), spec(2, bwd),
                  pl.BlockSpec(dmat.shape, lambda bi, i: (0, 0, 0)), pl.BlockSpec(tab.shape, lambda bi, i: (0, 0, 0, 0))],
        out_specs=[
            pl.BlockSpec((1, TB, B_WIDTH), lambda bi, i: (bi, i, 0)),
            pl.BlockSpec((1, TB, B_WIDTH), lambda bi, i: (bi, nb - 1 - i, 0)),
        ],
        out_shape=[jax.ShapeDtypeStruct((b, t, B_WIDTH), F32)] * 2,
        scratch_shapes=[
            pltpu.VMEM((2, B_HEADS // 2, LANES, LANES), F32),
        ],
        compiler_params=_params(("parallel", "arbitrary")),
        name="retention",
    )(bqkv, bqkv, bqkv, bqkv, bqkv, bqkv, dmat, tab)


def _attention_body(nq, q_ref, km, kp, kn, vm, vp, vn, gate_ref, sink_ref, o_ref, kx, vx, bias):
    TB = MIX_TB
    i = pl.program_id(1)
    nqb = TB // QBLOCK
    lo_k = lax.broadcasted_iota(jnp.int32, (TB + 2 * QBLOCK, LANES), 1) < HEAD_DIM
    for src, dst in (((kp, km, kn), kx), ((vp, vm, vn), vx)):
        full = jnp.concatenate([src[0][0], src[1][0], src[2][0]], axis=0)
        swapped = pltpu.roll(full, HEAD_DIM, 1)
        dst[0] = jnp.where(lo_k, full, swapped).astype(BF16)
        dst[1] = jnp.where(lo_k, swapped, full).astype(BF16)

    qi = lax.broadcasted_iota(jnp.int32, (QBLOCK, 3 * QBLOCK), 0)
    kj = lax.broadcasted_iota(jnp.int32, (QBLOCK, 3 * QBLOCK), 1)
    rel = jnp.abs(kj - QBLOCK - qi)
    relf = rel.astype(F32)
    for hq in range(C_HEADS):
        slope = 2.0 ** (-8.0 * (hq + 1) / C_HEADS)
        bias[hq] = jnp.where(rel <= WINDOW, -slope * relf, -jnp.inf)
    first_lo = jnp.where(i == 0, QBLOCK, 0)
    last_hi = jnp.where(i == nq - 1, 2 * QBLOCK, 3 * QBLOCK)
    sink_all = sink_ref[...]
    lo_q = lax.broadcasted_iota(jnp.int32, (QBLOCK, LANES), 1) < HEAD_DIM
    zero = jnp.zeros((QBLOCK, LANES), BF16)

    items = [(qb, g) for qb in range(nqb) for g in range(C_KV_HEADS)]

    def scores(item):
        qb, g = item
        q = (q_ref[0, qb * QBLOCK:(qb + 1) * QBLOCK, g * LANES:(g + 1) * LANES] * SCALE).astype(BF16)
        qm = jnp.concatenate([jnp.where(lo_q, q, zero), jnp.where(lo_q, zero, q)], axis=0)
        return _dot_nt(qm, kx[g, qb * QBLOCK:(qb + 3) * QBLOCK, :])

    for i0 in range(0, len(items), ATTN_GROUP):
        group = items[i0:i0 + ATTN_GROUP]
        s_all = [scores(item) for item in group]
        heads = [(n, j) for n in range(len(group)) for j in range(C_GROUP)]
        ss = []
        for n, j in heads:
            qb, g = group[n]
            s = s_all[n][j * QBLOCK:(j + 1) * QBLOCK] + bias[g * C_GROUP + j]
            if qb == 0:
                s = jnp.where(kj >= first_lo, s, -jnp.inf)
            if qb == nqb - 1:
                s = jnp.where(kj < last_hi, s, -jnp.inf)
            ss.append(s)
        sinks = [sink_all[:, group[n][1] * C_GROUP + j:group[n][1] * C_GROUP + j + 1] for n, j in heads]
        ms = [jnp.maximum(jnp.max(s, axis=-1, keepdims=True), sink) for s, sink in zip(ss, sinks)]
        ps = [jnp.exp(s - m) for s, m in zip(ss, ms)]
        dens = [jnp.sum(p, axis=-1, keepdims=True) + jnp.exp(sink - m) for p, sink, m in zip(ps, sinks, ms)]
        pbs = [p.astype(BF16) for p in ps]
        o2s = [_dot(jnp.concatenate(pbs[C_GROUP * n:C_GROUP * (n + 1)], axis=0), vx[g, qb * QBLOCK:(qb + 3) * QBLOCK, :])
               for n, (qb, g) in enumerate(group)]
        for n, (qb, g) in enumerate(group):
            o = jnp.where(lo_q, o2s[n][0:QBLOCK] * (1.0 / dens[2 * n]), o2s[n][QBLOCK:2 * QBLOCK] * (1.0 / dens[2 * n + 1]))
            rows = slice(qb * QBLOCK, (qb + 1) * QBLOCK)
            cols = slice(g * LANES, (g + 1) * LANES)
            o_ref[0, rows, cols] = o * _silu(gate_ref[0, rows, cols])


def _attention(proj, sink):
    TB = MIX_TB
    b, t, _ = proj.shape
    nq = t // TB
    per = TB // QBLOCK
    last = t // QBLOCK - 1

    def halo(col):
        cb = col // C_KV_WIDTH
        return [
            pl.BlockSpec((1, TB, C_KV_WIDTH), lambda bi, i: (bi, i, cb)),
            pl.BlockSpec((1, QBLOCK, C_KV_WIDTH), lambda bi, i: (bi, jnp.maximum(i * per - 1, 0), cb)),
            pl.BlockSpec((1, QBLOCK, C_KV_WIDTH), lambda bi, i: (bi, jnp.minimum((i + 1) * per, last), cb)),
        ]

    return pl.pallas_call(
        functools.partial(_attention_body, nq),
        grid=(b, nq),
        in_specs=[pl.BlockSpec((1, TB, C_WIDTH), lambda bi, i: (bi, i, P_CQ // C_WIDTH))] + halo(P_CK) + halo(P_CV) + [
            pl.BlockSpec((1, TB, C_WIDTH), lambda bi, i: (bi, i, P_CG // C_WIDTH)),
            pl.BlockSpec((1, LANES), lambda bi, i: (0, 0))],
        out_specs=pl.BlockSpec((1, TB, C_WIDTH), lambda bi, i: (bi, i, 0)),
        out_shape=jax.ShapeDtypeStruct((b, t, C_WIDTH), F32),
        scratch_shapes=[
            pltpu.VMEM((C_KV_HEADS, TB + 2 * QBLOCK, LANES), BF16),
            pltpu.VMEM((C_KV_HEADS, TB + 2 * QBLOCK, LANES), BF16),
            pltpu.VMEM((C_HEADS, QBLOCK, 3 * QBLOCK), F32),
        ],
        compiler_params=_params(("parallel", "parallel")),
        name="attention",
    )(proj, proj, proj, proj, proj, proj, proj, proj, sink)


def _out_proj_body(final, h_ref, af, ab, ag, bf, bb, bg, mc, p_ref, dng, rng, wo, wple, wpg, fg, o_ref):
    lo_lane = _pair_masks(TB)
    inv = 1.0 / HEAD_DIM

    def head_stat(x):
        outs = []
        for p in range(x.shape[-1] // LANES):
            xb = x[:, p * LANES:(p + 1) * LANES]
            s_lo = jnp.sum(jnp.where(lo_lane, xb, 0.0), axis=-1, keepdims=True)
            s_hi = jnp.sum(jnp.where(lo_lane, 0.0, xb), axis=-1, keepdims=True)
            outs.append(jnp.where(lo_lane, s_lo, s_hi) * inv)
        return jnp.concatenate(outs, axis=-1)

    oa = af[0] + ab[0]
    ya = oa * lax.rsqrt(head_stat(oa * oa) + EPS) * dng[...] * _silu(ag[0])
    ob = bf[0] + bb[0]
    xc = ob - head_stat(ob)
    yb = xc * lax.rsqrt(head_stat(xc * xc) + EPS) * rng[...] * _silu(bg[0])
    mix = jnp.concatenate([ya, yb, mc[0]], axis=-1).astype(BF16)
    h1 = h_ref[0] + _dot(mix, wo[...])
    h2 = h1 + _dot(p_ref[0, 0].astype(BF16), wple[...]) * _sigmoid(_dot(h1.astype(BF16), wpg[...]))
    if final:
        ms = jnp.mean(h2 * h2, axis=-1, keepdims=True)
        h2 = h2 * lax.rsqrt(ms + EPS) * fg[...]
    o_ref[0] = h2


def _out_proj(final, layer, h, oaf, oab, obf, obb, mixc, proj, p, dng, rng, wo, wple, wpg, fg):
    b, t, _ = h.shape
    row = lambda w: pl.BlockSpec((1, TB, w), lambda bi, i: (bi, i, 0))
    col = lambda w, c: pl.BlockSpec((1, TB, w), lambda bi, i: (bi, i, c // w))
    const = lambda shape: pl.BlockSpec(shape, lambda bi, i: (0, 0))
    return pl.pallas_call(
        functools.partial(_out_proj_body, final),
        grid=(b, t // TB),
        in_specs=[row(D_MODEL), row(A_WIDTH), row(A_WIDTH), col(A_WIDTH, P_AG), row(B_WIDTH), row(B_WIDTH), col(B_WIDTH, P_BG),
                  row(C_WIDTH), pl.BlockSpec((1, 1, TB, PLE_DIM), lambda bi, i: (layer, bi, i, 0)),
                  const((1, A_WIDTH)), const((1, B_WIDTH)), const((D_MODEL, D_MODEL)), const((PLE_DIM, D_MODEL)),
                  const((D_MODEL, D_MODEL)), const((1, D_MODEL))],
        out_specs=row(D_MODEL),
        out_shape=jax.ShapeDtypeStruct((b, t, D_MODEL), F32),
        compiler_params=_params(("parallel", "parallel")),
        name="out_proj",
    )(h, oaf, oab, proj, obf, obb, proj, mixc, p, dng, rng, wo, wple, wpg, fg)


def _layer_weights(i, w_in, w_out, norm_g, conv_w, dn_a_log, dn_dt_bias, dn_norm_g, ret_decay_z, ret_norm_g, attn_sink,
                   w_ple, w_pg):
    w = w_in[i]
    w_p = jnp.concatenate([w[:, :ORIG_BA], w[:, ORIG_BA + N_BA:], w[:, ORIG_BA:ORIG_BA + N_BA],
                           jnp.zeros((D_MODEL, LANES - N_BA), F32)], axis=-1).astype(BF16)
    w_ba_t = jnp.pad(w[:, ORIG_BA:ORIG_BA + N_BA].T, ((0, BA_ROWS - N_BA), (0, 0))).astype(BF16)
    a = jnp.exp(dn_a_log[i].astype(F32)).reshape(-1)
    dt = dn_dt_bias[i].astype(F32).reshape(-1)
    place = lambda x, n: jnp.zeros((n,), F32).at[G_OFF:G_OFF + 2 * A_HEADS].set(x)
    dmat, tab = _retention_tables(jnp.pad(ret_decay_z[i].astype(F32), ((0, SUBLANES - 2), (0, LANES - B_HEADS))))
    return dict(
        dmat=dmat, tab=tab,
        norm_g=norm_g[i].reshape(1, D_MODEL).astype(F32), w_p=w_p, w_ba_t=w_ba_t,
        cw=jnp.pad(conv_w[i].astype(F32), ((0, SUBLANES - CONV_K), (0, 0))),
        acol=place(a, LANES).reshape(1, LANES), dtcol=place(dt, LANES).reshape(1, LANES),
        arow=place(a, BA_ROWS).reshape(BA_ROWS, 1), dtrow=place(dt, BA_ROWS).reshape(BA_ROWS, 1),
        sink=jnp.pad(attn_sink[i].astype(F32), (0, LANES - C_HEADS)).reshape(1, LANES),
        dng=jnp.tile(dn_norm_g[i].astype(F32), A_HEADS).reshape(1, A_WIDTH),
        rng=ret_norm_g[i].astype(F32).reshape(1, B_WIDTH),
        wo=w_out[i].astype(BF16), wple=w_ple[i].astype(BF16), wpg=w_pg[i].astype(BF16),
    )


def _trunk(x, p, layers, final_g):
    h = x
    fg = final_g.reshape(1, D_MODEL).astype(F32)
    for i, lw in enumerate(layers):
        proj, gt, bqkv = _in_proj(h, lw["norm_g"], lw["w_p"], lw["w_ba_t"], lw["cw"])
        oaf, oab = _delta(proj, gt, lw["acol"], lw["dtcol"], lw["arow"], lw["dtrow"])
        obf, obb = _retention(bqkv, lw["dmat"], lw["tab"])
        mixc = _attention(proj, lw["sink"])
        h = _out_proj(i == len(layers) - 1, i, h, oaf, oab, obf, obb, mixc, proj, p, lw["dng"], lw["rng"],
                      lw["wo"], lw["wple"], lw["wpg"], fg)
    return h


def kernel(x_prompt, x_sample, p_prompt, p_sample, w_in, w_out, norm_g, conv_w, dn_a_log, dn_dt_bias, dn_norm_g,
           ret_decay_z, ret_norm_g, attn_sink, w_ple, w_pg, final_g):
    depth = w_in.shape[0]
    layers = [_layer_weights(i, w_in, w_out, norm_g, conv_w, dn_a_log, dn_dt_bias, dn_norm_g, ret_decay_z, ret_norm_g,
                             attn_sink, w_ple, w_pg) for i in range(depth)]
    y_prompt = _trunk(x_prompt, p_prompt, layers, final_g)
    y_sample = _trunk(x_sample, p_sample, layers, final_g)
    return (y_prompt, y_sample)
```

```python
import functools

import jax
import jax.numpy as jnp
from jax import lax
from jax.experimental import pallas as pl
from jax.experimental.pallas import tpu as pltpu

F32 = jnp.float32
BF16 = jnp.bfloat16

D_MODEL = 1024
HEAD_DIM = 64
A_HEADS = 6
A_WIDTH = A_HEADS * HEAD_DIM
B_HEADS = 6
B_WIDTH = B_HEADS * HEAD_DIM
C_HEADS = 4
C_KV_HEADS = 2
C_GROUP = C_HEADS // C_KV_HEADS
C_WIDTH = C_HEADS * HEAD_DIM
C_KV_WIDTH = C_KV_HEADS * HEAD_DIM
CONV_K = 5
WINDOW = 128
PLE_DIM = 256
EPS = 1e-6
SCALE = HEAD_DIM ** -0.5

LANES = 128
SUBLANES = 8
VMEM_LIMIT = 48 * 1024 * 1024
OUT_VMEM_LIMIT = 58 * 1024 * 1024

QKV_W = 3 * A_WIDTH
P_AG = QKV_W
P_BQ = P_AG + A_WIDTH
P_BK = P_BQ + B_WIDTH
P_BV = P_BK + B_WIDTH
P_BG = P_BV + B_WIDTH
P_CQ = P_BG + B_WIDTH
P_CK = P_CQ + C_WIDTH
P_CV = P_CK + C_KV_WIDTH
P_CG = P_CV + C_KV_WIDTH
P_BA = P_CG + C_WIDTH
PROJ_W = P_BA + LANES
ORIG_BA = QKV_W + A_WIDTH
N_BA = 4 * A_HEADS
BA_ROWS = 32
G_OFF = 2 * A_HEADS

CHUNK = 64
RCHUNK = 128
QBLOCK = 128
TB = 512
MIX_TB = 1024
DELTA_UNROLL = 2
ATTN_GROUP = 4

NT_DIMS = (((1,), (1,)), ((), ()))
TN_DIMS = (((0,), (0,)), ((), ()))


def _dot(a, b):
    return jnp.dot(a, b, preferred_element_type=F32)


def _dot_nt(a, b):
    return lax.dot_general(a, b, NT_DIMS, preferred_element_type=F32)


def _dot_tn(a, b):
    return lax.dot_general(a, b, TN_DIMS, preferred_element_type=F32)


def _sigmoid(x):
    return 0.5 * jnp.tanh(0.5 * x) + 0.5


def _silu(x):
    return x * _sigmoid(x)


def _softplus(x):
    return jnp.maximum(x, 0.0) + jnp.log1p(jnp.exp(-jnp.abs(x)))


def _split3(x):
    hi = x.astype(BF16)
    r1 = x - hi.astype(F32)
    mid = r1.astype(BF16)
    lo = (r1 - mid.astype(F32)).astype(BF16)
    return hi, mid, lo


def _pair_masks(rows):
    lane = lax.broadcasted_iota(jnp.int32, (rows, LANES), 1)
    return lane < HEAD_DIM


def _params(sem, flags=None, vmem=VMEM_LIMIT):
    return pltpu.CompilerParams(dimension_semantics=sem, vmem_limit_bytes=vmem, flags=flags)


def _in_proj_body(nb, h_ref, hp_ref, hn_ref, g_ref, w_ref, wba_ref, cw_ref, proj_ref, gt_ref, bqkv_ref, xs):
    i = pl.program_id(1)
    x = jnp.concatenate([hp_ref[0], h_ref[0], hn_ref[0]], axis=0)
    ms = jnp.mean(x * x, axis=-1, keepdims=True)
    xn_ext = (x * lax.rsqrt(ms + EPS) * g_ref[...]).astype(BF16)
    xn = xn_ext[SUBLANES:TB + SUBLANES]
    row = lax.broadcasted_iota(jnp.int32, (TB + 2 * SUBLANES, 1), 0)
    inside = ((row >= SUBLANES) | (i > 0)) & ((row < TB + SUBLANES) | (i < nb - 1))
    lo_lane = _pair_masks(CHUNK)
    cw = cw_ref[...]
    base = SUBLANES - CONV_K // 2
    for c0 in range(0, QKV_W, 512):
        c1 = min(c0 + 512, QKV_W)
        xs[:, c0:c1] = jnp.where(inside, _dot(xn_ext, w_ref[:, c0:c1]), 0.0)
    for c0 in range(QKV_W, PROJ_W, 512):
        c1 = min(c0 + 512, PROJ_W)
        val = _dot(xn, w_ref[:, c0:c1])
        proj_ref[0, :, c0:c1] = val
        s0, s1 = max(c0, P_BQ), min(c1, P_BG)
        if s0 < s1:
            bqkv_ref[0, :, s0 - P_BQ:s1 - P_BQ] = val[:, s0 - c0:s1 - c0].astype(BF16)
    gt_ref[0] = _dot_nt(wba_ref[...], xn)
    for part in range(3):
        c0 = part * A_WIDTH
        for r0 in range(0, TB, CHUNK):
            win = xs[r0:r0 + CHUNK + 2 * SUBLANES, c0:c0 + A_WIDTH]
            acc = None
            for k in range(CONV_K):
                shifted = win if k == CONV_K // 2 else pltpu.roll(win, (CONV_K // 2 - k) % (CHUNK + 2 * SUBLANES), 0)
                term = cw[k:k + 1, c0:c0 + A_WIDTH] * shifted[SUBLANES:SUBLANES + CHUNK]
                acc = term if acc is None else acc + term
            y = _silu(acc)
            if part < 2:
                for p in range(A_WIDTH // LANES):
                    y2 = y[:, p * LANES:(p + 1) * LANES]
                    sq = y2 * y2
                    s_lo = jnp.sum(jnp.where(lo_lane, sq, 0.0), axis=-1, keepdims=True)
                    s_hi = jnp.sum(jnp.where(lo_lane, 0.0, sq), axis=-1, keepdims=True)
                    r = jnp.where(lo_lane, lax.rsqrt(s_lo + EPS), lax.rsqrt(s_hi + EPS))
                    proj_ref[0, r0:r0 + CHUNK, c0 + p * LANES:c0 + (p + 1) * LANES] = y2 * r
            else:
                proj_ref[0, r0:r0 + CHUNK, c0:c0 + A_WIDTH] = y


def _in_proj(h, norm_g, w_p, w_ba_t, cw):
    b, t, _ = h.shape
    nb = t // TB
    hb = TB // SUBLANES
    last_h = t // SUBLANES - 1
    const = lambda shape: pl.BlockSpec(shape, lambda bi, i: (0, 0))
    return pl.pallas_call(
        functools.partial(_in_proj_body, nb),
        grid=(b, nb),
        in_specs=[
            pl.BlockSpec((1, TB, D_MODEL), lambda bi, i: (bi, i, 0)),
            pl.BlockSpec((1, SUBLANES, D_MODEL), lambda bi, i: (bi, jnp.maximum(i * hb - 1, 0), 0)),
            pl.BlockSpec((1, SUBLANES, D_MODEL), lambda bi, i: (bi, jnp.minimum((i + 1) * hb, last_h), 0)),
            const((1, D_MODEL)), const((D_MODEL, PROJ_W)), const((BA_ROWS, D_MODEL)), const((SUBLANES, QKV_W)),
        ],
        out_specs=[
            pl.BlockSpec((1, TB, PROJ_W), lambda bi, i: (bi, i, 0)),
            pl.BlockSpec((1, BA_ROWS, TB), lambda bi, i: (bi, 0, i)),
            pl.BlockSpec((1, TB, 3 * B_WIDTH), lambda bi, i: (bi, i, 0)),
        ],
        out_shape=[jax.ShapeDtypeStruct((b, t, PROJ_W), F32), jax.ShapeDtypeStruct((b, BA_ROWS, t), F32),
                   jax.ShapeDtypeStruct((b, t, 3 * B_WIDTH), BF16)],
        scratch_shapes=[pltpu.VMEM((TB + 2 * SUBLANES, QKV_W), F32)],
        compiler_params=_params(("parallel", "parallel")),
        name="in_proj",
    )(h, h, h, norm_g, w_p, w_ba_t, cw)


def _delta_body(nb, *refs):
    TB = MIX_TB
    (qkv_f, ba_f, gt_f, qkv_b, ba_b, gt_b, acol_ref, dtcol_ref, arow_ref, dtrow_ref,
     of_ref, ob_ref, bcol, dcol, drow, st) = refs
    i = pl.program_id(1)
    nc = TB // CHUNK
    npair = A_HEADS // 2

    @pl.when(i == 0)
    def _():
        st[...] = jnp.zeros_like(st)

    ri = lax.broadcasted_iota(jnp.int32, (CHUNK, CHUNK), 0)
    ci = lax.broadcasted_iota(jnp.int32, (CHUNK, CHUNK), 1)
    incl = (ri >= ci, ri <= ci)

    ins = ((qkv_f, ba_f, gt_f, of_ref), (qkv_b, ba_b, gt_b, ob_ref))
    for dirn, (_, ba, gt, _) in enumerate(ins):
        cum = incl[dirn].astype(BF16)
        cum_t = incl[1 - dirn].astype(BF16)
        for c in range(nc):
            r0 = c * CHUNK
            bac = ba[0, r0:r0 + CHUNK, :]
            bcol[dirn, r0:r0 + CHUNK, :] = _sigmoid(bac)
            g_c = -acol_ref[...] * _softplus(bac + dtcol_ref[...])
            h3, m3, l3 = _split3(g_c)
            d3 = _dot(cum, jnp.concatenate([h3, m3, l3], axis=-1))
            dcol[dirn, r0:r0 + CHUNK, :] = (d3[:, 0:LANES] + d3[:, LANES:2 * LANES]) + d3[:, 2 * LANES:3 * LANES]
            gtc = gt[0, :, r0:r0 + CHUNK]
            g_r = -arow_ref[...] * _softplus(gtc + dtrow_ref[...])
            h3, m3, l3 = _split3(g_r)
            d3 = _dot(jnp.concatenate([h3, m3, l3], axis=0), cum_t)
            drow[dirn, c] = (d3[0:BA_ROWS] + d3[BA_ROWS:2 * BA_ROWS]) + d3[2 * BA_ROWS:3 * BA_ROWS]

    ri2 = lax.broadcasted_iota(jnp.int32, (CHUNK, LANES), 0)
    ln2 = lax.broadcasted_iota(jnp.int32, (CHUNK, LANES), 1)
    lo = ln2 < HEAD_DIM
    eye_hi = jnp.where(ri2 == ln2 - HEAD_DIM, 1.0, 0.0)
    keep_qk = (lo & (ri2 >= ln2), lo & (ri2 <= ln2))
    keep_m = (lo & (ri2 > ln2), lo & (ri2 < ln2))
    rs2 = lax.broadcasted_iota(jnp.int32, (LANES, LANES), 0)
    cs2 = lax.broadcasted_iota(jnp.int32, (LANES, LANES), 1)
    same_head = (rs2 < HEAD_DIM) == (cs2 < HEAD_DIM)
    lo_row = lax.broadcasted_iota(jnp.int32, (1, LANES), 1) < HEAD_DIM
    zero_b = jnp.zeros((CHUNK, LANES), BF16)

    def chunk_group(cg, carry):
        pairs, chains = [], []
        for j in range(DELTA_UNROLL):
            c = cg * DELTA_UNROLL + j
            for dirn, (qkv, _, _, _) in enumerate(ins):
                cc = c if dirn == 0 else nc - 1 - c
                r0 = pl.multiple_of(cc * CHUNK, CHUNK)
                beta_all = bcol[dirn, pl.ds(r0, CHUNK), :]
                dcol_all = dcol[dirn, pl.ds(r0, CHUNK), :]
                drow_all = drow[dirn, cc]
                for p in range(npair):
                    cols = lambda part: slice(part * A_WIDTH + p * LANES, part * A_WIDTH + (p + 1) * LANES)
                    q2 = qkv[0, pl.ds(r0, CHUNK), cols(0)]
                    k2 = qkv[0, pl.ds(r0, CHUNK), cols(1)]
                    v2 = qkv[0, pl.ds(r0, CHUNK), cols(2)]
                    pair = dict(j=j, dirn=dirn, p=p, r0=r0, k2=k2, v2=v2, qb=q2.astype(BF16), kb=k2.astype(BF16))
                    pairs.append(pair)
                    for e in range(2):
                        lb = A_HEADS * dirn + 2 * p + e
                        lg = G_OFF + lb
                        dr = drow_all[lg:lg + 1, :]
                        chains.append(dict(
                            pair=pair, e=e, dirn=dirn, own=lo if e == 0 else ~lo,
                            beta=jnp.broadcast_to(beta_all[:, lb:lb + 1], (CHUNK, LANES)),
                            dc=jnp.broadcast_to(dcol_all[:, lg:lg + 1], (CHUNK, LANES)),
                            dr2=jnp.concatenate([dr, dr], axis=-1),
                            dl=dr[:, CHUNK - 1:CHUNK] if dirn == 0 else dr[:, 0:1]))
        for x in chains:
            km = jnp.where(x["own"], x["pair"]["kb"], zero_b)
            qm = jnp.where(x["own"], x["pair"]["qb"], zero_b)
            x["gp"] = _dot_nt(jnp.concatenate([km, qm], axis=0), jnp.concatenate([km, km], axis=0))
        pms = []
        for x in chains:
            gam = jnp.exp(x["dc"] - x["dr2"])
            x["e_dc"] = jnp.exp(x["dc"])
            gp = x["gp"]
            pms.append(jnp.where(keep_m[x["dirn"]], -(gp[0:CHUNK] * x["beta"] * gam), eye_hi))
            x["qk"] = jnp.where(keep_qk[x["dirn"]], gp[CHUNK:2 * CHUNK] * SCALE * gam, 0.0)[:, 0:HEAD_DIM].astype(BF16)
            rhs = jnp.concatenate([x["pair"]["v2"] * x["beta"], x["pair"]["k2"] * (x["beta"] * x["e_dc"])], axis=-1)
            x["rhs"] = jnp.concatenate([jnp.zeros((CHUNK, 2 * LANES), BF16), rhs.astype(BF16)], axis=0)
        for s in range(6):
            pmbs = [pm.astype(BF16) for pm in pms]
            mrs = [_dot(pmb[:, 0:HEAD_DIM], pmb) for pmb in pmbs]
            pms = [jnp.where(lo, mr, pm + mr) for pm, mr in zip(pms, mrs)]
        uws = [_dot(pm.astype(BF16), x["rhs"]) for pm, x in zip(pms, chains)]
        per_chunk = 2 * npair
        for j in range(DELTA_UNROLL):
            prs = pairs[j * per_chunk:(j + 1) * per_chunk]
            chs = chains[2 * j * per_chunk:2 * (j + 1) * per_chunk]
            uwj = uws[2 * j * per_chunk:2 * (j + 1) * per_chunk]
            sbs = [st[x["dirn"], x["p"]] for x in prs]
            sbbs = [s_old.astype(BF16) for s_old in sbs]
            wss = [_dot(uw[:, LANES:2 * LANES].astype(BF16), sbbs[n // 2]) for n, uw in enumerate(uwj)]
            qss = [_dot(x["qb"], sbb) for x, sbb in zip(prs, sbbs)]
            vns = [uw[:, 0:LANES] - ws for uw, ws in zip(uwj, wss)]
            o2s = [_dot(x["qk"], vn.astype(BF16)) for x, vn in zip(chs, vns)]
            vss = [vn * jnp.exp(x["dl"] - x["dc"]) for x, vn in zip(chs, vns)]
            for n, x in enumerate(prs):
                a, b2 = chs[2 * n], chs[2 * n + 1]
                tn = _dot_tn(x["kb"], jnp.where(lo, vss[2 * n], vss[2 * n + 1]).astype(BF16))
                cd = jnp.where(lo_row, jnp.exp(a["dl"]), jnp.exp(b2["dl"]))
                st[x["dirn"], x["p"]] = sbs[n] * cd + jnp.where(same_head, tn, 0.0)
                o = jnp.where(lo, o2s[2 * n] + qss[n] * (SCALE * a["e_dc"]),
                              o2s[2 * n + 1] + qss[n] * (SCALE * b2["e_dc"]))
                ins[x["dirn"]][3][0, pl.ds(x["r0"], CHUNK), x["p"] * LANES:(x["p"] + 1) * LANES] = o
        return carry

    lax.fori_loop(0, nc // DELTA_UNROLL, chunk_group, 0)


def _delta(proj, gt, acol, dtcol, arow, dtrow):
    TB = MIX_TB
    b, t, _ = proj.shape
    nb = t // TB
    ba_blk = P_BA // LANES

    def specs(blk_of):
        return [
            pl.BlockSpec((1, TB, QKV_W), lambda bi, i: (bi, blk_of(i), 0)),
            pl.BlockSpec((1, TB, LANES), lambda bi, i: (bi, blk_of(i), ba_blk)),
            pl.BlockSpec((1, BA_ROWS, TB), lambda bi, i: (bi, 0, blk_of(i))),
        ]

    fwd = lambda i: i
    bwd = lambda i: nb - 1 - i
    const = lambda shape: pl.BlockSpec(shape, lambda bi, i: (0, 0))
    return pl.pallas_call(
        functools.partial(_delta_body, nb),
        grid=(b, nb),
        in_specs=specs(fwd) + specs(bwd) + [const((1, LANES)), const((1, LANES)), const((BA_ROWS, 1)), const((BA_ROWS, 1))],
        out_specs=[
            pl.BlockSpec((1, TB, A_WIDTH), lambda bi, i: (bi, i, 0)),
            pl.BlockSpec((1, TB, A_WIDTH), lambda bi, i: (bi, nb - 1 - i, 0)),
        ],
        out_shape=[jax.ShapeDtypeStruct((b, t, A_WIDTH), F32)] * 2,
        scratch_shapes=[
            pltpu.VMEM((2, TB, LANES), F32),
            pltpu.VMEM((2, TB, LANES), F32),
            pltpu.VMEM((2, TB // CHUNK, BA_ROWS, CHUNK), F32),
            pltpu.VMEM((2, A_HEADS // 2, LANES, LANES), F32),
        ],
        compiler_params=_params(("parallel", "arbitrary")),
        name="delta",
    )(proj, proj, gt, proj, proj, gt, acol, dtcol, arow, dtrow)


def _retention_tables_body(z_ref, dmat, tab):
    lg = -_softplus(-z_ref[...])
    ri = lax.broadcasted_iota(jnp.int32, (RCHUNK, RCHUNK), 0)
    ci = lax.broadcasted_iota(jnp.int32, (RCHUNK, RCHUNK), 1)
    diff = (ri - ci).astype(F32)
    pos = ri.astype(F32)
    lo = ci < HEAD_DIM
    for h in range(B_HEADS):
        lgf = lg[0:1, h:h + 1]
        lgb = lg[1:2, h:h + 1]
        dmat[h] = SCALE * (jnp.where(ri >= ci, jnp.exp(diff * lgf), 0.0) + jnp.where(ci >= ri, jnp.exp(-diff * lgb), 0.0))
    for p in range(B_HEADS // 2):
        lgf = jnp.where(lo, lg[0:1, 2 * p:2 * p + 1], lg[0:1, 2 * p + 1:2 * p + 2])
        lgb = jnp.where(lo, lg[1:2, 2 * p:2 * p + 1], lg[1:2, 2 * p + 1:2 * p + 2])
        tab[p, 0] = SCALE * jnp.exp((pos + 1.0) * lgf)
        tab[p, 1] = jnp.exp((RCHUNK - 1.0 - pos) * lgf)
        tab[p, 2] = SCALE * jnp.exp((RCHUNK - pos) * lgb)
        tab[p, 3] = jnp.exp(pos * lgb)
        tab[p, 4] = jnp.exp(RCHUNK * lgf)
        tab[p, 5] = jnp.exp(RCHUNK * lgb)


def _retention_tables(z):
    return pl.pallas_call(
        _retention_tables_body,
        out_shape=[jax.ShapeDtypeStruct((B_HEADS, RCHUNK, RCHUNK), F32),
                   jax.ShapeDtypeStruct((B_HEADS // 2, 6, RCHUNK, LANES), F32)],
        name="retention_tables",
    )(z)


def _retention_body(nb, qf, kf, vf, qb_, kb_, vb_, dmat, tab, of_ref, ob_ref, st):
    TB = MIX_TB
    i = pl.program_id(1)
    nc = TB // RCHUNK
    npair = B_HEADS // 2

    @pl.when(i == 0)
    def _():
        st[...] = jnp.zeros_like(st)

    ri = lax.broadcasted_iota(jnp.int32, (RCHUNK, RCHUNK), 0)
    ci = lax.broadcasted_iota(jnp.int32, (RCHUNK, RCHUNK), 1)
    lo = ci < HEAD_DIM
    same_head = (ri < HEAD_DIM) == lo

    def chunk_step(c, carry):
        rf = pl.multiple_of(c * RCHUNK, RCHUNK)
        rb = pl.multiple_of((nc - 1 - c) * RCHUNK, RCHUNK)
        loads = []
        for p in range(npair):
            cols = slice(p * LANES, (p + 1) * LANES)
            loads.append((qf[0, pl.ds(rf, RCHUNK), cols].astype(BF16), kf[0, pl.ds(rf, RCHUNK), cols].astype(BF16),
                          vf[0, pl.ds(rf, RCHUNK), cols], qb_[0, pl.ds(rb, RCHUNK), cols].astype(BF16),
                          kb_[0, pl.ds(rb, RCHUNK), cols].astype(BF16), vb_[0, pl.ds(rb, RCHUNK), cols]))
        zero = jnp.zeros((RCHUNK, LANES), BF16)
        s_lo = [_dot_nt(x[0], jnp.where(lo, x[1], zero)) for x in loads]
        s_hi = [_dot_nt(x[0], jnp.where(lo, zero, x[1])) for x in loads]
        qsf = [_dot(x[0], st[0, p].astype(BF16)) for p, x in enumerate(loads)]
        qsb = [_dot(x[3], st[1, p].astype(BF16)) for p, x in enumerate(loads)]
        tnf = [_dot_tn(x[1], (x[2] * tab[p, 1]).astype(BF16)) for p, x in enumerate(loads)]
        tnb = [_dot_tn(x[4], (x[5] * tab[p, 3]).astype(BF16)) for p, x in enumerate(loads)]
        intra = []
        for p, x in enumerate(loads):
            vb16 = x[2].astype(BF16)
            pm = jnp.concatenate([(s_lo[p] * dmat[2 * p]).astype(BF16), (s_hi[p] * dmat[2 * p + 1]).astype(BF16)], axis=-1)
            v2 = jnp.concatenate([jnp.where(lo, vb16, zero), jnp.where(lo, zero, vb16)], axis=0)
            intra.append(_dot(pm, v2))
        for p in range(npair):
            cols = slice(p * LANES, (p + 1) * LANES)
            of_ref[0, pl.ds(rf, RCHUNK), cols] = intra[p] + qsf[p] * tab[p, 0]
            ob_ref[0, pl.ds(rb, RCHUNK), cols] = qsb[p] * tab[p, 2]
            st[0, p] = st[0, p] * tab[p, 4] + jnp.where(same_head, tnf[p], 0.0)
            st[1, p] = st[1, p] * tab[p, 5] + jnp.where(same_head, tnb[p], 0.0)
        return carry

    lax.fori_loop(0, nc, chunk_step, 0)


def _retention(bqkv, dmat, tab):
    TB = MIX_TB
    b, t, _ = bqkv.shape
    nb = t // TB

    def spec(part, blk_of):
        return pl.BlockSpec((1, TB, B_WIDTH), lambda bi, i: (bi, blk_of(i), part))

    fwd = lambda i: i
    bwd = lambda i: nb - 1 - i
    return pl.pallas_call(
        functools.partial(_retention_body, nb),
        grid=(b, nb),
        in_specs=[spec(0, fwd), spec(1, fwd), spec(2, fwd), spec(0, bwd), spec(1, bwd), spec(2, bwd),
                  pl.BlockSpec(dmat.shape, lambda bi, i: (0, 0, 0)), pl.BlockSpec(tab.shape, lambda bi, i: (0, 0, 0, 0))],
        out_specs=[
            pl.BlockSpec((1, TB, B_WIDTH), lambda bi, i: (bi, i, 0)),
            pl.BlockSpec((1, TB, B_WIDTH), lambda bi, i: (bi, nb - 1 - i, 0)),
        ],
        out_shape=[jax.ShapeDtypeStruct((b, t, B_WIDTH), F32)] * 2,
        scratch_shapes=[
            pltpu.VMEM((2, B_HEADS // 2, LANES, LANES), F32),
        ],
        compiler_params=_params(("parallel", "arbitrary")),
        name="retention",
    )(bqkv, bqkv, bqkv, bqkv, bqkv, bqkv, dmat, tab)


def _attention_body(nq, q_ref, km, kp, kn, vm, vp, vn, gate_ref, sink_ref, o_ref, kx, vx, bias):
    TB = MIX_TB
    i = pl.program_id(1)
    nqb = TB // QBLOCK
    lo_k = lax.broadcasted_iota(jnp.int32, (TB + 2 * QBLOCK, LANES), 1) < HEAD_DIM
    for src, dst in (((kp, km, kn), kx), ((vp, vm, vn), vx)):
        full = jnp.concatenate([src[0][0], src[1][0], src[2][0]], axis=0)
        swapped = pltpu.roll(full, HEAD_DIM, 1)
        dst[0] = jnp.where(lo_k, full, swapped).astype(BF16)
        dst[1] = jnp.where(lo_k, swapped, full).astype(BF16)

    qi = lax.broadcasted_iota(jnp.int32, (QBLOCK, 3 * QBLOCK), 0)
    kj = lax.broadcasted_iota(jnp.int32, (QBLOCK, 3 * QBLOCK), 1)
    rel = jnp.abs(kj - QBLOCK - qi)
    relf = rel.astype(F32)
    for hq in range(C_HEADS):
        slope = 2.0 ** (-8.0 * (hq + 1) / C_HEADS)
        bias[hq] = jnp.where(rel <= WINDOW, -slope * relf, -jnp.inf)
    first_lo = jnp.where(i == 0, QBLOCK, 0)
    last_hi = jnp.where(i == nq - 1, 2 * QBLOCK, 3 * QBLOCK)
    sink_all = sink_ref[...]
    lo_q = lax.broadcasted_iota(jnp.int32, (QBLOCK, LANES), 1) < HEAD_DIM
    zero = jnp.zeros((QBLOCK, LANES), BF16)

    items = [(qb, g) for qb in range(nqb) for g in range(C_KV_HEADS)]

    def scores(item):
        qb, g = item
        q = (q_ref[0, qb * QBLOCK:(qb + 1) * QBLOCK, g * LANES:(g + 1) * LANES] * SCALE).astype(BF16)
        qm = jnp.concatenate([jnp.where(lo_q, q, zero), jnp.where(lo_q, zero, q)], axis=0)
        return _dot_nt(qm, kx[g, qb * QBLOCK:(qb + 3) * QBLOCK, :])

    for i0 in range(0, len(items), ATTN_GROUP):
        group = items[i0:i0 + ATTN_GROUP]
        s_all = [scores(item) for item in group]
        heads = [(n, j) for n in range(len(group)) for j in range(C_GROUP)]
        ss = []
        for n, j in heads:
            qb, g = group[n]
            s = s_all[n][j * QBLOCK:(j + 1) * QBLOCK] + bias[g * C_GROUP + j]
            if qb == 0:
                s = jnp.where(kj >= first_lo, s, -jnp.inf)
            if qb == nqb - 1:
                s = jnp.where(kj < last_hi, s, -jnp.inf)
            ss.append(s)
        sinks = [sink_all[:, group[n][1] * C_GROUP + j:group[n][1] * C_GROUP + j + 1] for n, j in heads]
        ms = [jnp.maximum(jnp.max(s, axis=-1, keepdims=True), sink) for s, sink in zip(ss, sinks)]
        ps = [jnp.exp(s - m) for s, m in zip(ss, ms)]
        dens = [jnp.sum(p, axis=-1, keepdims=True) + jnp.exp(sink - m) for p, sink, m in zip(ps, sinks, ms)]
        pbs = [p.astype(BF16) for p in ps]
        o2s = [_dot(jnp.concatenate(pbs[C_GROUP * n:C_GROUP * (n + 1)], axis=0), vx[g, qb * QBLOCK:(qb + 3) * QBLOCK, :])
               for n, (qb, g) in enumerate(group)]
        for n, (qb, g) in enumerate(group):
            o = jnp.where(lo_q, o2s[n][0:QBLOCK] * (1.0 / dens[2 * n]), o2s[n][QBLOCK:2 * QBLOCK] * (1.0 / dens[2 * n + 1]))
            rows = slice(qb * QBLOCK, (qb + 1) * QBLOCK)
            cols = slice(g * LANES, (g + 1) * LANES)
            o_ref[0, rows, cols] = o * _silu(gate_ref[0, rows, cols])


def _attention(proj, sink):
    TB = MIX_TB
    b, t, _ = proj.shape
    nq = t // TB
    per = TB // QBLOCK
    last = t // QBLOCK - 1

    def halo(col):
        cb = col // C_KV_WIDTH
        return [
            pl.BlockSpec((1, TB, C_KV_WIDTH), lambda bi, i: (bi, i, cb)),
            pl.BlockSpec((1, QBLOCK, C_KV_WIDTH), lambda bi, i: (bi, jnp.maximum(i * per - 1, 0), cb)),
            pl.BlockSpec((1, QBLOCK, C_KV_WIDTH), lambda bi, i: (bi, jnp.minimum((i + 1) * per, last), cb)),
        ]

    return pl.pallas_call(
        functools.partial(_attention_body, nq),
        grid=(b, nq),
        in_specs=[pl.BlockSpec((1, TB, C_WIDTH), lambda bi, i: (bi, i, P_CQ // C_WIDTH))] + halo(P_CK) + halo(P_CV) + [
            pl.BlockSpec((1, TB, C_WIDTH), lambda bi, i: (bi, i, P_CG // C_WIDTH)),
            pl.BlockSpec((1, LANES), lambda bi, i: (0, 0))],
        out_specs=pl.BlockSpec((1, TB, C_WIDTH), lambda bi, i: (bi, i, 0)),
        out_shape=jax.ShapeDtypeStruct((b, t, C_WIDTH), F32),
        scratch_shapes=[
            pltpu.VMEM((C_KV_HEADS, TB + 2 * QBLOCK, LANES), BF16),
            pltpu.VMEM((C_KV_HEADS, TB + 2 * QBLOCK, LANES), BF16),
            pltpu.VMEM((C_HEADS, QBLOCK, 3 * QBLOCK), F32),
        ],
        compiler_params=_params(("parallel", "parallel")),
        name="attention",
    )(proj, proj, proj, proj, proj, proj, proj, proj, sink)


def _out_proj_body(final, h_ref, af, ab, ag, bf, bb, bg, mc, p_ref, dng, rng, wo, wple, wpg, fg, o_ref):
    TB = MIX_TB
    lo_lane = _pair_masks(TB)
    inv = 1.0 / HEAD_DIM

    def head_stat(x):
        outs = []
        for p in range(x.shape[-1] // LANES):
            xb = x[:, p * LANES:(p + 1) * LANES]
            s_lo = jnp.sum(jnp.where(lo_lane, xb, 0.0), axis=-1, keepdims=True)
            s_hi = jnp.sum(jnp.where(lo_lane, 0.0, xb), axis=-1, keepdims=True)
            outs.append(jnp.where(lo_lane, s_lo, s_hi) * inv)
        return jnp.concatenate(outs, axis=-1)

    oa = af[0] + ab[0]
    ya = oa * lax.rsqrt(head_stat(oa * oa) + EPS) * dng[...] * _silu(ag[0])
    ob = bf[0] + bb[0]
    xc = ob - head_stat(ob)
    yb = xc * lax.rsqrt(head_stat(xc * xc) + EPS) * rng[...] * _silu(bg[0])
    mix = jnp.concatenate([ya, yb, mc[0]], axis=-1).astype(BF16)
    h1 = h_ref[0] + _dot(mix, wo[...])
    h2 = h1 + _dot(p_ref[0, 0].astype(BF16), wple[...]) * _sigmoid(_dot(h1.astype(BF16), wpg[...]))
    if final:
        ms = jnp.mean(h2 * h2, axis=-1, keepdims=True)
        h2 = h2 * lax.rsqrt(ms + EPS) * fg[...]
    o_ref[0] = h2


def _out_proj(final, layer, h, oaf, oab, obf, obb, mixc, proj, p, dng, rng, wo, wple, wpg, fg):
    TB = MIX_TB
    b, t, _ = h.shape
    row = lambda w: pl.BlockSpec((1, TB, w), lambda bi, i: (bi, i, 0))
    col = lambda w, c: pl.BlockSpec((1, TB, w), lambda bi, i: (bi, i, c // w))
    const = lambda shape: pl.BlockSpec(shape, lambda bi, i: (0, 0), pipeline_mode=pl.Buffered(1))
    return pl.pallas_call(
        functools.partial(_out_proj_body, final),
        grid=(b, t // TB),
        in_specs=[row(D_MODEL), row(A_WIDTH), row(A_WIDTH), col(A_WIDTH, P_AG), row(B_WIDTH), row(B_WIDTH), col(B_WIDTH, P_BG),
                  row(C_WIDTH), pl.BlockSpec((1, 1, TB, PLE_DIM), lambda bi, i: (layer, bi, i, 0)),
                  const((1, A_WIDTH)), const((1, B_WIDTH)), const((D_MODEL, D_MODEL)), const((PLE_DIM, D_MODEL)),
                  const((D_MODEL, D_MODEL)), const((1, D_MODEL))],
        out_specs=row(D_MODEL),
        out_shape=jax.ShapeDtypeStruct((b, t, D_MODEL), F32),
        compiler_params=_params(("parallel", "parallel"), vmem=OUT_VMEM_LIMIT),
        name="out_proj",
    )(h, oaf, oab, proj, obf, obb, proj, mixc, p, dng, rng, wo, wple, wpg, fg)


def _layer_weights(i, w_in, w_out, norm_g, conv_w, dn_a_log, dn_dt_bias, dn_norm_g, ret_decay_z, ret_norm_g, attn_sink,
                   w_ple, w_pg):
    w = w_in[i]
    w_p = jnp.concatenate([w[:, :ORIG_BA], w[:, ORIG_BA + N_BA:], w[:, ORIG_BA:ORIG_BA + N_BA],
                           jnp.zeros((D_MODEL, LANES - N_BA), F32)], axis=-1).astype(BF16)
    w_ba_t = jnp.pad(w[:, ORIG_BA:ORIG_BA + N_BA].T, ((0, BA_ROWS - N_BA), (0, 0))).astype(BF16)
    a = jnp.exp(dn_a_log[i].astype(F32)).reshape(-1)
    dt = dn_dt_bias[i].astype(F32).reshape(-1)
    place = lambda x, n: jnp.zeros((n,), F32).at[G_OFF:G_OFF + 2 * A_HEADS].set(x)
    dmat, tab = _retention_tables(jnp.pad(ret_decay_z[i].astype(F32), ((0, SUBLANES - 2), (0, LANES - B_HEADS))))
    return dict(
        dmat=dmat, tab=tab,
        norm_g=norm_g[i].reshape(1, D_MODEL).astype(F32), w_p=w_p, w_ba_t=w_ba_t,
        cw=jnp.pad(conv_w[i].astype(F32), ((0, SUBLANES - CONV_K), (0, 0))),
        acol=place(a, LANES).reshape(1, LANES), dtcol=place(dt, LANES).reshape(1, LANES),
        arow=place(a, BA_ROWS).reshape(BA_ROWS, 1), dtrow=place(dt, BA_ROWS).reshape(BA_ROWS, 1),
        sink=jnp.pad(attn_sink[i].astype(F32), (0, LANES - C_HEADS)).reshape(1, LANES),
        dng=jnp.tile(dn_norm_g[i].astype(F32), A_HEADS).reshape(1, A_WIDTH),
        rng=ret_norm_g[i].astype(F32).reshape(1, B_WIDTH),
        wo=w_out[i].astype(BF16), wple=w_ple[i].astype(BF16), wpg=w_pg[i].astype(BF16),
    )


def _trunk(x, p, layers, final_g):
    h = x
    fg = final_g.reshape(1, D_MODEL).astype(F32)
    for i, lw in enumerate(layers):
        proj, gt, bqkv = _in_proj(h, lw["norm_g"], lw["w_p"], lw["w_ba_t"], lw["cw"])
        oaf, oab = _delta(proj, gt, lw["acol"], lw["dtcol"], lw["arow"], lw["dtrow"])
        obf, obb = _retention(bqkv, lw["dmat"], lw["tab"])
        mixc = _attention(proj, lw["sink"])
        h = _out_proj(i == len(layers) - 1, i, h, oaf, oab, obf, obb, mixc, proj, p, lw["dng"], lw["rng"],
                      lw["wo"], lw["wple"], lw["wpg"], fg)
    return h


def kernel(x_prompt, x_sample, p_prompt, p_sample, w_in, w_out, norm_g, conv_w, dn_a_log, dn_dt_bias, dn_norm_g,
           ret_decay_z, ret_norm_g, attn_sink, w_ple, w_pg, final_g):
    depth = w_in.shape[0]
    layers = [_layer_weights(i, w_in, w_out, norm_g, conv_w, dn_a_log, dn_dt_bias, dn_norm_g, ret_decay_z, ret_norm_g,
                             attn_sink, w_ple, w_pg) for i in range(depth)]
    y_prompt = _trunk(x_prompt, p_prompt, layers, final_g)
    y_sample = _trunk(x_sample, p_sample, layers, final_g)
    return (y_prompt, y_sample)
```
